```python
import jax, jax.numpy as jnp
from jax import lax
import numpy as np

D_MODEL = 1024
BATCH = 4
SEQ = 4096
DEPTH = 1

GRID_W = 64
CTX_LEN = 256
N_HEADS = 16
Q_RANK = 256
KV_RANK = 128
NOPE_DIM = 64
ROPE_DIM = 32
V_DIM = 64
ROPE_THETA = 10000.0
Q_BLOCK = 128
ATTN_SCALE = (NOPE_DIM + ROPE_DIM) ** -0.5
CONV_CH = D_MODEL
CONV_WIDTH = 31
CONV_HALF = CONV_WIDTH // 2
P_HEADS = 8
N_KEYS = 128
N_EXPERTS = N_KEYS * N_KEYS
P_DK = 256
P_TOPK = 16
P_CHUNK = 128
EPS = 1e-6

OFF_Q = 0
OFF_KV = Q_RANK
OFF_CONV = Q_RANK + KV_RANK + ROPE_DIM
OFF_GATE = OFF_CONV + 2 * CONV_CH
N_IN = OFF_GATE + 2 * D_MODEL

kernel_name = "hybrid_mla_conformer_peer_dit_layer"


def rms_norm(x, g):
    xf = x.astype(jnp.float32)
    y = xf * lax.rsqrt(jnp.mean(xf * xf, axis=-1, keepdims=True) + EPS)
    return (y * g.astype(jnp.float32)).astype(x.dtype)


def layer_norm(x, g, b):
    xf = x.astype(jnp.float32)
    mu = jnp.mean(xf, axis=-1, keepdims=True)
    var = jnp.mean(jnp.square(xf - mu), axis=-1, keepdims=True)
    y = (xf - mu) * lax.rsqrt(var + EPS) * g.astype(jnp.float32) + b.astype(jnp.float32)
    return y.astype(x.dtype)


def modulate(h, shift, scale):
    return h * (1 + scale) + shift


def axial_rope_tables(n_tok):
    rows = n_tok // GRID_W
    r = jnp.repeat(jnp.arange(rows, dtype=jnp.int32), GRID_W)
    col = jnp.tile(jnp.arange(GRID_W, dtype=jnp.int32), rows)
    n_freq = ROPE_DIM // 4
    freq = ROPE_THETA ** (-jnp.arange(n_freq, dtype=jnp.float32) / n_freq)
    ang = jnp.stack([r[:, None] * freq, col[:, None] * freq], axis=1)
    return jnp.cos(ang), jnp.sin(ang)


def apply_rope(x, cos, sin):
    shp = x.shape
    xr = x.reshape(shp[:-1] + (2, 2, ROPE_DIM // 4)).astype(jnp.float32)
    extra = x.ndim - 3
    c = cos.reshape(cos.shape[:1] + (1,) * extra + cos.shape[1:])
    s = sin.reshape(sin.shape[:1] + (1,) * extra + sin.shape[1:])
    x1, x2 = xr[..., 0, :], xr[..., 1, :]
    out = jnp.stack([x1 * c - x2 * s, x2 * c + x1 * s], axis=-2)
    return out.reshape(shp).astype(x.dtype)


def mla_queries(cq, q_norm_g, w_uq):
    b, n, _ = cq.shape
    q = (rms_norm(cq, q_norm_g) @ w_uq).reshape(b, n, N_HEADS, NOPE_DIM + ROPE_DIM)
    return q[..., :NOPE_DIM], q[..., NOPE_DIM:]


def mla_keys(kvr, kv_norm_g, w_ukv):
    b, n, _ = kvr.shape
    c_kv = rms_norm(kvr[..., :KV_RANK], kv_norm_g)
    kv = (c_kv @ w_ukv).reshape(b, n, N_HEADS, NOPE_DIM + V_DIM)
    return kv[..., :NOPE_DIM], kvr[..., KV_RANK:], kv[..., NOPE_DIM:]


def attend(qn, qr, kn, kr, v):
    s = jnp.einsum('bqhd,bkhd->bhqk', qn, kn) + jnp.einsum('bqhd,bkd->bhqk', qr, kr)
    p = jax.nn.softmax(s.astype(jnp.float32) * ATTN_SCALE, axis=-1).astype(v.dtype)
    return jnp.einsum('bhqk,bkhd->bqhd', p, v)


def attend_blocked(qn, qr, kn, kr, v):
    b, n = qn.shape[:2]
    nb = n // Q_BLOCK
    blk = lambda t: jnp.moveaxis(t.reshape((b, nb, Q_BLOCK) + t.shape[2:]), 1, 0)
    out = lax.map(lambda qs: attend(qs[0], qs[1], kn, kr, v), (blk(qn), blk(qr)))
    return jnp.moveaxis(out, 0, 1).reshape(b, n, N_HEADS, V_DIM)


def conformer_conv(u, conv_w, conv_b, ln_g, ln_b, w_pw):
    a, g = jnp.split(u, 2, axis=-1)
    y = a * jax.nn.sigmoid(g)
    y = lax.conv_general_dilated(y, conv_w[:, None, :], window_strides=(1,),
                                 padding=[(CONV_HALF, CONV_HALF)],
                                 dimension_numbers=('NWC', 'WIO', 'NWC'),
                                 feature_group_count=CONV_CH) + conv_b
    y = jax.nn.silu(layer_norm(y, ln_g, ln_b))
    return y @ w_pw


def merge_branches(u, attn_heads, conv_w, conv_b, ln_g, ln_b, w_pw, w_o_mla, w_out):
    b, n = attn_heads.shape[:2]
    y_a = attn_heads.reshape(b, n, N_HEADS * V_DIM) @ w_o_mla
    y_c = conformer_conv(u[..., OFF_CONV:OFF_GATE], conv_w, conv_b, ln_g, ln_b, w_pw)
    g = jax.nn.sigmoid(u[..., OFF_GATE:])
    return (g[..., :D_MODEL] * y_a + g[..., D_MODEL:] * y_c) @ w_out


def peer_ffn(h, w_pq, sub_keys, u_tab, v_tab):
    shp = h.shape
    toks = h.reshape(-1, P_CHUNK, D_MODEL)

    def chunk(xc):
        q = (xc @ w_pq).reshape(P_CHUNK, P_HEADS, 2, P_DK // 2)
        s = jnp.einsum('thpd,hpkd->thpk', q, sub_keys)
        s_top, i_top = lax.top_k(s, P_TOPK)
        cand = s_top[..., 0, :, None] + s_top[..., 1, None, :]
        best, ci = lax.top_k(cand.reshape(P_CHUNK, P_HEADS, P_TOPK * P_TOPK), P_TOPK)
        e1 = jnp.take_along_axis(i_top[..., 0, :], ci // P_TOPK, axis=-1)
        e2 = jnp.take_along_axis(i_top[..., 1, :], ci % P_TOPK, axis=-1)
        expert = e1 * N_KEYS + e2
        gate = jax.nn.softmax(best.astype(jnp.float32), axis=-1).astype(xc.dtype)
        act = jax.nn.gelu(jnp.einsum('thkd,td->thk', u_tab[expert], xc), approximate=False)
        return jnp.einsum('thk,thkd->td', gate * act, v_tab[expert])

    return lax.map(chunk, toks).reshape(shp)


def setup_inputs(seed: int = 0) -> dict:
    key = jax.random.key(seed)
    ks = jax.random.split(key, 30)
    nrm = lambda k, s, sc: jax.random.normal(k, s, jnp.float32) * sc
    D = D_MODEL
    return {
        "x": nrm(ks[0], (BATCH, SEQ, D), 1.0),
        "c": nrm(ks[1], (BATCH, D), 1.0),
        "ctx": nrm(ks[2], (BATCH, CTX_LEN, D), 1.0),
        "c_ctx": nrm(ks[3], (D,), 1.0),
        "w_mod": nrm(ks[4], (DEPTH, D, 6 * D), 0.5 * D ** -0.5),
        "b_mod": nrm(ks[5], (DEPTH, 6 * D), 0.01),
        "norm1_g": 1.0 + nrm(ks[6], (DEPTH, D), 0.02),
        "norm2_g": 1.0 + nrm(ks[7], (DEPTH, D), 0.02),
        "w_in": nrm(ks[8], (DEPTH, D, N_IN), D ** -0.5),
        "q_norm_g": 1.0 + nrm(ks[9], (DEPTH, Q_RANK), 0.02),
        "w_uq": nrm(ks[10], (DEPTH, Q_RANK, N_HEADS * (NOPE_DIM + ROPE_DIM)), Q_RANK ** -0.5),
        "kv_norm_g": 1.0 + nrm(ks[11], (DEPTH, KV_RANK), 0.02),
        "w_ukv": nrm(ks[12], (DEPTH, KV_RANK, N_HEADS * (NOPE_DIM + V_DIM)), KV_RANK ** -0.5),
        "w_o_mla": nrm(ks[13], (DEPTH, N_HEADS * V_DIM, D), (N_HEADS * V_DIM) ** -0.5),
        "conv_w": nrm(ks[14], (DEPTH, CONV_WIDTH, CONV_CH), CONV_WIDTH ** -0.5),
        "conv_b": nrm(ks[15], (DEPTH, CONV_CH), 0.01),
        "conv_ln_g": 1.0 + nrm(ks[16], (DEPTH, CONV_CH), 0.02),
        "conv_ln_b": nrm(ks[17], (DEPTH, CONV_CH), 0.01),
        "w_pw": nrm(ks[18], (DEPTH, CONV_CH, D), CONV_CH ** -0.5),
        "w_out": nrm(ks[19], (DEPTH, D, D), D ** -0.5),
        "w_pq": nrm(ks[20], (DEPTH, D, P_HEADS * P_DK), D ** -0.5),
        "sub_keys": nrm(ks[21], (DEPTH, P_HEADS, 2, N_KEYS, P_DK // 2), (P_DK // 2) ** -0.5),
        "u_experts": nrm(ks[22], (DEPTH, N_EXPERTS, D), D ** -0.5),
        "v_experts": nrm(ks[23], (DEPTH, N_EXPERTS, D), 1.0),
        "final_g": 1.0 + nrm(ks[24], (D,), 0.02),
    }


def reference(x, c, ctx, c_ctx, w_mod, b_mod, norm1_g, norm2_g, w_in, q_norm_g, w_uq,
              kv_norm_g, w_ukv, w_o_mla, conv_w, conv_b, conv_ln_g, conv_ln_b, w_pw,
              w_out, w_pq, sub_keys, u_experts, v_experts, final_g):
    n_lat = x.shape[1]
    cos, sin = axial_rope_tables(n_lat)
    for l in range(DEPTH):
        last = l == DEPTH - 1
        mod_lat = (jax.nn.silu(c) @ w_mod[l] + b_mod[l])[:, None, :]
        mod_ctx = jax.nn.silu(c_ctx) @ w_mod[l] + b_mod[l]
        sh1, sc1, g1, sh2, sc2, g2 = jnp.split(mod_lat, 6, axis=-1)
        csh1, csc1, cg1, csh2, csc2, cg2 = jnp.split(mod_ctx, 6, axis=-1)

        h = modulate(rms_norm(x, norm1_g[l]), sh1, sc1)
        u = h @ w_in[l]
        qn, qr = mla_queries(u[..., OFF_Q:OFF_KV], q_norm_g[l], w_uq[l])
        qr = apply_rope(qr, cos, sin)
        kn_l, kr_l, v_l = mla_keys(u[..., OFF_KV:OFF_CONV], kv_norm_g[l], w_ukv[l])
        kr_l = apply_rope(kr_l, cos, sin)

        hc = modulate(rms_norm(ctx, norm1_g[l]), csh1, csc1)
        if last:
            uc = hc @ w_in[l][:, OFF_KV:OFF_CONV]
            kvr_c = uc
        else:
            uc = hc @ w_in[l]
            kvr_c = uc[..., OFF_KV:OFF_CONV]
        kn_c, kr_c, v_c = mla_keys(kvr_c, kv_norm_g[l], w_ukv[l])

        kn = jnp.concatenate([kn_l, kn_c], axis=1)
        kr = jnp.concatenate([kr_l, kr_c], axis=1)
        vv = jnp.concatenate([v_l, v_c], axis=1)
        attn_lat = attend_blocked(qn, qr, kn, kr, vv)
        x = x + g1 * merge_branches(u, attn_lat, conv_w[l], conv_b[l], conv_ln_g[l],
                                    conv_ln_b[l], w_pw[l], w_o_mla[l], w_out[l])
        if not last:
            qn_c, qr_c = mla_queries(uc[..., OFF_Q:OFF_KV], q_norm_g[l], w_uq[l])
            attn_c = attend(qn_c, qr_c, kn_c, kr_c, v_c)
            ctx = ctx + cg1 * merge_branches(uc, attn_c, conv_w[l], conv_b[l], conv_ln_g[l],
                                             conv_ln_b[l], w_pw[l], w_o_mla[l], w_out[l])

        h2 = modulate(rms_norm(x, norm2_g[l]), sh2, sc2)
        x = x + g2 * peer_ffn(h2, w_pq[l], sub_keys[l], u_experts[l], v_experts[l])
        if not last:
            hc2 = modulate(rms_norm(ctx, norm2_g[l]), csh2, csc2)
            ctx = ctx + cg2 * peer_ffn(hc2, w_pq[l], sub_keys[l], u_experts[l], v_experts[l])
    return rms_norm(x, final_g)
```

```python
import functools

import jax
import jax.numpy as jnp
import numpy as np
from jax import lax
from jax.experimental import pallas as pl
from jax.experimental.pallas import tpu as pltpu

F32 = jnp.float32
BF16 = jnp.bfloat16

EPS = 1e-6
GRID_W = 64
N_HEADS = 16
NOPE_DIM = 64
ROPE_DIM = 32
V_DIM = 64
ROPE_THETA = 10000.0
ATTN_SCALE = (NOPE_DIM + ROPE_DIM) ** -0.5
CONV_WIDTH = 31
CONV_HALF = CONV_WIDTH // 2
P_HEADS = 8
P_TOPK = 16

LANES = 128
SUBLANES = 8
HEAD_PAD = LANES
HALO = 16
VMEM_LIMIT = 56 * 1024 * 1024

_NT = (((1,), (1,)), ((), ()))


def _dot(a, b):
    return jnp.dot(a, b, preferred_element_type=F32)


def _dot_nt(a, b):
    return lax.dot_general(a, b, _NT, preferred_element_type=F32)


def _rms(x):
    return x * lax.rsqrt(jnp.mean(x * x, axis=-1, keepdims=True) + EPS)


def _cparams(sem):
    return pltpu.CompilerParams(dimension_semantics=sem, vmem_limit_bytes=VMEM_LIMIT)


def _mod_kernel(c_ref, w_ref, b_ref, o_ref):
    cv = c_ref[...]
    s = cv * jax.nn.sigmoid(cv)
    o_ref[...] = jnp.dot(s, w_ref[...], preferred_element_type=F32,
                         precision=lax.Precision.HIGHEST) + b_ref[...]


def _mod_call(cvec, w_mod, b_mod):
    rows, d = cvec.shape
    n = w_mod.shape[1]
    tn = d
    return pl.pallas_call(
        _mod_kernel,
        grid=(n // tn,),
        in_specs=[pl.BlockSpec((rows, d), lambda j: (0, 0)),
                  pl.BlockSpec((d, tn), lambda j: (0, j)),
                  pl.BlockSpec((1, tn), lambda j: (0, j))],
        out_specs=pl.BlockSpec((rows, tn), lambda j: (0, j)),
        out_shape=jax.ShapeDtypeStruct((rows, n), F32),
        compiler_params=_cparams(("arbitrary",)),
        name="mod",
    )(cvec, w_mod, b_mod)


def _modulated_norm(x, g, shift, scale):
    return (_rms(x) * g) * (1.0 + scale) + shift


def _kv_from_h(hb, wkvc_ref, kvg_ref, wukv_ref):
    ckv = _dot(hb, wkvc_ref[...])
    ckvn = (_rms(ckv) * kvg_ref[...]).astype(BF16)
    return _dot(ckvn, wukv_ref[...])


def _inproj_kernel(x_ref, mod_ref, g1_ref, wq_ref, wkvc_ref, wkr_ref, wkrs_ref, wca_ref, wcg_ref,
                   wgate_ref, qg_ref, wuq_ref, wuqs_ref, kvg_ref, wukv_ref,
                   cq_ref, sq_ref, ck_ref, sk_ref,
                   q_out, k_out, v_out, y_out, gate_out):
    d = x_ref.shape[-1]
    hp = N_HEADS * HEAD_PAD
    m = mod_ref[0]
    hb = _modulated_norm(x_ref[0], g1_ref[...], m[:, 0:d], m[:, d:2 * d]).astype(BF16)

    cqn = (_rms(_dot(hb, wq_ref[...])) * qg_ref[...]).astype(BF16)
    qa = _dot(cqn, wuq_ref[...])
    qb = _dot(cqn, wuqs_ref[...])
    cq, sq = cq_ref[...], sq_ref[...]
    for h in range(N_HEADS):
        sl = slice(h * HEAD_PAD, (h + 1) * HEAD_PAD)
        q_out[0, :, sl] = (qa[:, sl] * cq + qb[:, sl] * sq).astype(BF16)

    kv = _kv_from_h(hb, wkvc_ref, kvg_ref, wukv_ref)
    kr = _dot(hb, wkr_ref[...]) * ck_ref[...] + _dot(hb, wkrs_ref[...]) * sk_ref[...]
    for h in range(N_HEADS):
        sl = slice(h * HEAD_PAD, (h + 1) * HEAD_PAD)
        k_out[0, :, sl] = (kv[:, sl] + kr).astype(BF16)
    v_out[0] = kv[:, hp:].astype(BF16)

    a = _dot(hb, wca_ref[...])
    g = _dot(hb, wcg_ref[...])
    y_out[0] = (a * jax.nn.sigmoid(g)).astype(BF16)
    gate_out[0] = jax.nn.sigmoid(_dot(hb, wgate_ref[...])).astype(BF16)


def _const_spec(arr):
    nd = arr.ndim
    return pl.BlockSpec(arr.shape, lambda *_: (0,) * nd)


def _inproj_call(x, mod3, g1, wts, tabs, tm):
    b, l, d = x.shape
    hp = N_HEADS * HEAD_PAD
    nt = l // tm
    weights = [wts[k] for k in ("wq", "wkvc", "wkr", "wkrs", "wca", "wcg", "wgate",
                                "qg", "wuq", "wuqs", "kvg", "wukv")]
    tok = lambda w: pl.BlockSpec((1, tm, w), lambda bi, i: (bi, i, 0))
    tab = pl.BlockSpec((tm, HEAD_PAD), lambda bi, i: (i, 0))
    return pl.pallas_call(
        _inproj_kernel,
        grid=(b, nt),
        in_specs=[tok(d),
                  pl.BlockSpec((1, 1, mod3.shape[-1]), lambda bi, i: (bi, 0, 0)),
                  _const_spec(g1)] + [_const_spec(w) for w in weights] + [tab] * 4,
        out_specs=[tok(hp), tok(hp), tok(hp), tok(d), tok(2 * d)],
        out_shape=[jax.ShapeDtypeStruct((b, l, hp), BF16),
                   jax.ShapeDtypeStruct((b, l, hp), BF16),
                   jax.ShapeDtypeStruct((b, l, hp), BF16),
                   jax.ShapeDtypeStruct((b, l, d), BF16),
                   jax.ShapeDtypeStruct((b, l, 2 * d), BF16)],
        compiler_params=_cparams(("parallel", "parallel")),
        name="in_proj",
    )(x, mod3, g1, *weights, tabs["cq"], tabs["sq"], tabs["ck"], tabs["sk"])


def _ctxkv_kernel(x_ref, mod_ref, g1_ref, wkvc_ref, wkr_ref, kvg_ref, wukv_ref, k_out, v_out):
    d = x_ref.shape[-1]
    hp = N_HEADS * HEAD_PAD
    m = mod_ref[0]
    hb = _modulated_norm(x_ref[0], g1_ref[...], m[:, 0:d], m[:, d:2 * d]).astype(BF16)
    kv = _kv_from_h(hb, wkvc_ref, kvg_ref, wukv_ref)
    kr = _dot(hb, wkr_ref[...])
    for h in range(N_HEADS):
        sl = slice(h * HEAD_PAD, (h + 1) * HEAD_PAD)
        k_out[0, :, sl] = (kv[:, sl] + kr).astype(BF16)
    v_out[0] = kv[:, hp:].astype(BF16)


def _ctxkv_call(ctx, mod3, ctx_row, g1, wts):
    b, n, d = ctx.shape
    hp = N_HEADS * HEAD_PAD
    weights = [wts[k] for k in ("wkvc", "wkr", "kvg", "wukv")]
    tok = lambda w: pl.BlockSpec((1, n, w), lambda bi: (bi, 0, 0))
    return pl.pallas_call(
        _ctxkv_kernel,
        grid=(b,),
        in_specs=[tok(d),
                  pl.BlockSpec((1, 1, mod3.shape[-1]), lambda bi: (ctx_row, 0, 0)),
                  _const_spec(g1)] + [_const_spec(w) for w in weights],
        out_specs=[tok(hp), tok(hp)],
        out_shape=[jax.ShapeDtypeStruct((b, n, hp), BF16)] * 2,
        compiler_params=_cparams(("parallel",)),
        name="ctx_kv",
    )(ctx, mod3, g1, *weights)


def _attn_kernel(q_ref, kl_ref, vl_ref, kc_ref, vc_ref, o_ref, *, tk):
    tq = q_ref.shape[1]
    n_chunks = kl_ref.shape[1] // tk
    out = None
    for head in range(2):
        sl = slice(head * HEAD_PAD, (head + 1) * HEAD_PAD)
        q = q_ref[0, :, sl]

        def step(k, v, carry, q=q):
            m, l, acc = carry
            s = _dot_nt(q, k)
            m_new = jnp.maximum(m, jnp.max(s, axis=-1, keepdims=True))
            p = jnp.exp(s - m_new)
            alpha = jnp.exp(m - m_new)
            l = alpha * l + jnp.sum(p, axis=-1, keepdims=True)
            acc = alpha * acc + _dot(p.astype(BF16), v)
            return m_new, l, acc

        def body(c, carry, sl=sl, step=step):
            off = pl.multiple_of(c * tk, tk)
            return step(kl_ref[0, pl.ds(off, tk), sl], vl_ref[0, pl.ds(off, tk), sl], carry)

        carry = (jnp.full((tq, 1), -jnp.inf, F32), jnp.zeros((tq, 1), F32),
                 jnp.zeros((tq, HEAD_PAD), F32))
        carry = lax.fori_loop(0, n_chunks, body, carry)
        _, l, acc = step(kc_ref[0, :, sl], vc_ref[0, :, sl], carry)
        o = acc * (1.0 / l)
        out = o if out is None else out + o
    o_ref[0] = out.astype(BF16)


def _attn_call(q, kl, vl, kc, vc, tq, tk):
    b, l, hp = q.shape
    nc = kc.shape[1]
    pairs = N_HEADS // 2
    pw = 2 * HEAD_PAD
    return pl.pallas_call(
        functools.partial(_attn_kernel, tk=tk),
        grid=(b, pairs, l // tq),
        in_specs=[pl.BlockSpec((1, tq, pw), lambda bi, p, i: (bi, i, p)),
                  pl.BlockSpec((1, l, pw), lambda bi, p, i: (bi, 0, p)),
                  pl.BlockSpec((1, l, pw), lambda bi, p, i: (bi, 0, p)),
                  pl.BlockSpec((1, nc, pw), lambda bi, p, i: (bi, 0, p)),
                  pl.BlockSpec((1, nc, pw), lambda bi, p, i: (bi, 0, p))],
        out_specs=pl.BlockSpec((1, tq, HEAD_PAD), lambda bi, p, i: (bi, i, p)),
        out_shape=jax.ShapeDtypeStruct((b, l, pairs * HEAD_PAD), BF16),
        compiler_params=_cparams(("parallel", "parallel", "parallel")),
        name="attention",
    )(q, kl, vl, kc, vc)


def _merge_kernel(x_ref, mod_ref, attn_ref, y_ref, yp_ref, yn_ref, gate_ref,
                  womla_ref, wpw_ref, wout_ref, cw_ref, cb_ref, lng_ref, lnb_ref,
                  o_ref, ybuf, sbuf, cbuf, *, rows):
    i = pl.program_id(1)
    last = pl.num_programs(1) - 1
    tm, d = y_ref.shape[1], y_ref.shape[2]

    prev = yp_ref[0].astype(F32)
    nxt = yn_ref[0].astype(F32)
    ybuf[0:HALO, :] = jnp.where(i > 0, prev, 0.0)
    ybuf[HALO:HALO + tm, :] = y_ref[0].astype(F32)
    ybuf[HALO + tm:2 * HALO + tm, :] = jnp.where(i < last, nxt, 0.0)
    span = sbuf.shape[1]
    for p in range(SUBLANES):
        sbuf[p] = ybuf[p:p + span, :]

    cb = cb_ref[...]

    def conv_rows(r, _):
        base = pl.multiple_of(r * rows, rows)
        acc = jnp.broadcast_to(cb, (rows, d))
        for k in range(CONV_WIDTH):
            q, p = divmod(HALO - CONV_HALF + k, SUBLANES)
            acc = acc + cw_ref[k:k + 1, :] * sbuf[p, pl.ds(base + q * SUBLANES, rows), :]
        cbuf[pl.ds(base, rows), :] = acc
        return 0

    lax.fori_loop(0, tm // rows, conv_rows, 0)

    yc = cbuf[...]
    mu = jnp.mean(yc, axis=-1, keepdims=True)
    cen = yc - mu
    var = jnp.mean(cen * cen, axis=-1, keepdims=True)
    ln = cen * lax.rsqrt(var + EPS) * lng_ref[...] + lnb_ref[...]
    act = (ln * jax.nn.sigmoid(ln)).astype(BF16)
    y_c = _dot(act, wpw_ref[...])
    y_a = _dot(attn_ref[0], womla_ref[...])
    gate = gate_ref[0]
    z = gate[:, :d].astype(F32) * y_a + gate[:, d:].astype(F32) * y_c
    zo = _dot(z.astype(BF16), wout_ref[...])
    m = mod_ref[0]
    o_ref[0] = x_ref[0] + m[:, 2 * d:3 * d] * zo


def _merge_call(x, mod3, attn, y, gate, wts, tm):
    b, l, d = x.shape
    nt = l // tm
    hb = tm // HALO
    nhb = l // HALO
    weights = [wts[k] for k in ("womla", "wpw", "wout", "cw", "cb", "lng", "lnb")]
    tok = lambda w: pl.BlockSpec((1, tm, w), lambda bi, i: (bi, i, 0))
    return pl.pallas_call(
        functools.partial(_merge_kernel, rows=32),
        grid=(b, nt),
        in_specs=[tok(d),
                  pl.BlockSpec((1, 1, mod3.shape[-1]), lambda bi, i: (bi, 0, 0)),
                  tok(d), tok(d),
                  pl.BlockSpec((1, HALO, d), lambda bi, i: (bi, jnp.maximum(i * hb - 1, 0), 0)),
                  pl.BlockSpec((1, HALO, d), lambda bi, i: (bi, jnp.minimum((i + 1) * hb, nhb - 1), 0)),
                  tok(2 * d)] + [_const_spec(w) for w in weights],
        out_specs=tok(d),
        out_shape=jax.ShapeDtypeStruct((b, l, d), F32),
        scratch_shapes=[pltpu.VMEM((tm + 2 * HALO, d), F32),
                        pltpu.VMEM((SUBLANES, tm + 2 * HALO - SUBLANES, d), F32),
                        pltpu.VMEM((tm, d), F32)],
        compiler_params=_cparams(("parallel", "parallel")),
        name="merge",
    )(x, mod3, attn, y, y, y, gate, *weights)


def _topk_rank(s):
    nk = s.shape[0]
    key_id = lax.broadcasted_iota(jnp.int32, s.shape, 0)
    rank = jnp.full(s.shape, float(P_TOPK), F32)
    top_id = lax.broadcasted_iota(jnp.int32, (P_TOPK, s.shape[1]), 0)
    tops = jnp.zeros((P_TOPK, s.shape[1]), F32)
    for r in range(P_TOPK):
        cur = jnp.max(s, axis=0, keepdims=True)
        first = jnp.min(jnp.where(s == cur, key_id, nk), axis=0, keepdims=True)
        hit = key_id == first
        rank = jnp.where(hit, float(r), rank)
        s = jnp.where(hit, -jnp.inf, s)
        tops = jnp.where(top_id == r, cur, tops)
    return tops, rank


def _staircase(a, b):
    row_id = lax.broadcasted_iota(jnp.int32, a.shape, 0)
    cnt = jnp.zeros(a.shape, F32)
    front = a + b[0:1]
    for _ in range(P_TOPK):
        best = jnp.max(front, axis=0, keepdims=True)
        pick = jnp.min(jnp.where(front == best, row_id, P_TOPK), axis=0, keepdims=True)
        hit = row_id == pick
        cnt = jnp.where(hit, cnt + 1.0, cnt)
        nxt = jnp.full(a.shape, -jnp.inf, F32)
        for k in range(1, P_TOPK):
            nxt = jnp.where(cnt == float(k), b[k:k + 1], nxt)
        front = jnp.where(hit, a + nxt, front)
    return cnt


def _peer_kernel(x_ref, mod_ref, g2_ref, wpqt_ref, keys_ref, u_ref, vt_ref, fg_ref, o_ref,
                 h2_ref, sa_ref, rc_ref, top_ref, at_ref, gt_ref, acc_ref):
    j = pl.program_id(1)
    n_et = pl.num_programs(1)
    tt, d = x_ref.shape
    nlt = tt // LANES
    nk = keys_ref.shape[1]
    et = u_ref.shape[0]
    ke = et // nk
    m = mod_ref[0]

    @pl.when(j == 0)
    def _prepare():
        h2 = _modulated_norm(x_ref[...], g2_ref[...], m[:, 3 * d:4 * d], m[:, 4 * d:5 * d]).astype(BF16)
        h2_ref[...] = h2
        dk = keys_ref.shape[2]
        qt = _dot_nt(wpqt_ref[...], h2).astype(BF16)
        for hp in range(2 * P_HEADS):
            s = _dot(keys_ref[hp], qt[hp * dk:(hp + 1) * dk, :])
            for lt in range(nlt):
                sa_ref[hp, lt] = s[:, lt * LANES:(lt + 1) * LANES]

        def rank_unit(u, _):
            hp, lt = u // nlt, u % nlt
            tops, rank = _topk_rank(sa_ref[hp, lt])
            top_ref[hp, lt] = tops
            rc_ref[hp, lt] = rank
            return 0

        lax.fori_loop(0, 2 * P_HEADS * nlt, rank_unit, 0)

        def gate_unit(u, _):
            h, lt = u // nlt, u % nlt
            a, b = top_ref[2 * h, lt], top_ref[2 * h + 1, lt]
            cnt = _staircase(a, b)
            ea = jnp.exp(a - a[0:1])
            eb = jnp.exp(b - b[0:1])
            cum = jnp.zeros(a.shape, F32)
            for k in range(P_TOPK):
                cum = cum + jnp.where(cnt > float(k), eb[k:k + 1], 0.0)
            inv_z = 1.0 / jnp.sum(ea * cum, axis=0, keepdims=True)
            r1 = rc_ref[2 * h, lt]
            cnt1 = jnp.zeros(r1.shape, F32)
            for k in range(P_TOPK):
                cnt1 = jnp.where(r1 == float(k), cnt[k:k + 1], cnt1)
            rc_ref[2 * h, lt] = cnt1
            sa_ref[2 * h, lt] = jnp.exp(sa_ref[2 * h, lt] - a[0:1]) * inv_z
            sa_ref[2 * h + 1, lt] = jnp.exp(sa_ref[2 * h + 1, lt] - b[0:1])
            return 0

        lax.fori_loop(0, P_HEADS * nlt, gate_unit, 0)
        acc_ref[...] = jnp.zeros(acc_ref.shape, F32)

    at_ref[...] = _dot_nt(u_ref[...], h2_ref[...])

    def weight_rows(r, _):
        e1 = j * ke + r
        base = pl.multiple_of(r * nk, nk)
        for lt in range(nlt):
            ls = slice(lt * LANES, (lt + 1) * LANES)
            w = jnp.zeros((nk, LANES), F32)
            for h in range(P_HEADS):
                cnt_row = rc_ref[2 * h, lt, pl.ds(e1, 1), :]
                a_row = sa_ref[2 * h, lt, pl.ds(e1, 1), :]
                w = w + jnp.where(rc_ref[2 * h + 1, lt] < cnt_row, a_row * sa_ref[2 * h + 1, lt], 0.0)
            pre = at_ref[pl.ds(base, nk), ls]
            act = 0.5 * pre * (1.0 + lax.erf(pre * np.float32(2.0 ** -0.5)))
            gt_ref[pl.ds(base, nk), ls] = (act * w).astype(BF16)
        return 0

    lax.fori_loop(0, ke, weight_rows, 0)
    acc_ref[...] += _dot(vt_ref[...], gt_ref[...])

    @pl.when(j == n_et - 1)
    def _finish():
        x2 = x_ref[...] + m[:, 5 * d:6 * d] * acc_ref[...].T
        o_ref[...] = _rms(x2) * fg_ref[...]


def _peer_call(x1, mod3, g2, wpqt, keys, u_b, vt_b, fg, tokens_per_batch, tt, et):
    n, d = x1.shape
    ne = u_b.shape[0]
    nk = keys.shape[1]
    nlt = tt // LANES
    tiles_per_batch = tokens_per_batch // tt
    unit = (2 * P_HEADS, nlt, nk, LANES)
    return pl.pallas_call(
        _peer_kernel,
        grid=(n // tt, ne // et),
        in_specs=[pl.BlockSpec((tt, d), lambda t, j: (t, 0)),
                  pl.BlockSpec((1, 1, mod3.shape[-1]), lambda t, j: (t // tiles_per_batch, 0, 0)),
                  _const_spec(g2), _const_spec(wpqt), _const_spec(keys),
                  pl.BlockSpec((et, d), lambda t, j: (j, 0)),
                  pl.BlockSpec((d, et), lambda t, j: (0, j)),
                  _const_spec(fg)],
        out_specs=pl.BlockSpec((tt, d), lambda t, j: (t, 0)),
        out_shape=jax.ShapeDtypeStruct((n, d), F32),
        scratch_shapes=[pltpu.VMEM((tt, d), BF16),
                        pltpu.VMEM(unit, F32),
                        pltpu.VMEM(unit, F32),
                        pltpu.VMEM((2 * P_HEADS, nlt, P_TOPK, LANES), F32),
                        pltpu.VMEM((et, tt), F32),
                        pltpu.VMEM((et, tt), BF16),
                        pltpu.VMEM((d, tt), F32)],
        compiler_params=_cparams(("parallel", "arbitrary")),
        name="peer",
    )(x1, mod3, g2, wpqt, keys, u_b, vt_b, fg)


def _rope_tables(n_tok):
    rows = n_tok // GRID_W
    r = jnp.repeat(jnp.arange(rows, dtype=jnp.int32), GRID_W)
    col = jnp.tile(jnp.arange(GRID_W, dtype=jnp.int32), rows)
    n_freq = ROPE_DIM // 4
    freq = ROPE_THETA ** (-jnp.arange(n_freq, dtype=F32) / n_freq)
    ang = jnp.stack([r[:, None] * freq, col[:, None] * freq], axis=1)
    dd = np.arange(ROPE_DIM)
    axis, half, fr = dd // (ROPE_DIM // 2), (dd % (ROPE_DIM // 2)) // n_freq, dd % n_freq
    c32 = jnp.cos(ang)[:, axis, fr]
    s32 = jnp.sin(ang)[:, axis, fr] * jnp.where(half == 0, -1.0, 1.0).astype(F32)
    pad = jnp.zeros((n_tok, HEAD_PAD - NOPE_DIM - ROPE_DIM), F32)
    ones = jnp.ones((n_tok, NOPE_DIM), F32)
    ck = jnp.concatenate([ones, c32, pad], axis=1)
    sk = jnp.concatenate([0.0 * ones, s32, pad], axis=1)
    return {"cq": ck * ATTN_SCALE, "sq": sk * ATTN_SCALE, "ck": ck, "sk": sk}


def _rope_partner():
    dd = np.arange(ROPE_DIM)
    n_freq = ROPE_DIM // 4
    half = (dd % (ROPE_DIM // 2)) // n_freq
    return np.where(half == 0, dd + n_freq, dd - n_freq)


def _prep_weights(w_in, q_norm_g, w_uq, kv_norm_g, w_ukv, w_o_mla, conv_w, conv_b, conv_ln_g,
                  conv_ln_b, w_pw, w_out):
    d = w_in.shape[0]
    q_rank = q_norm_g.shape[0]
    kv_rank = kv_norm_g.shape[0]
    off_kv = q_rank
    off_kr = q_rank + kv_rank
    off_conv = off_kr + ROPE_DIM
    off_gate = off_conv + 2 * d
    partner = _rope_partner()
    qk = NOPE_DIM + ROPE_DIM
    lo, hi = NOPE_DIM, NOPE_DIM + ROPE_DIM

    def place_rope(w32):
        return jnp.zeros((w32.shape[0], HEAD_PAD), F32).at[:, lo:hi].set(w32)

    wkr32 = w_in[:, off_kr:off_conv]
    uq = w_uq.reshape(q_rank, N_HEADS, qk)
    uq_pad = jnp.zeros((q_rank, N_HEADS, HEAD_PAD), F32).at[:, :, :qk].set(uq)
    uq_sw = jnp.zeros((q_rank, N_HEADS, HEAD_PAD), F32).at[:, :, lo:hi].set(uq[:, :, NOPE_DIM:][:, :, partner])
    ukv = w_ukv.reshape(kv_rank, N_HEADS, NOPE_DIM + V_DIM)
    uk_pad = jnp.zeros((kv_rank, N_HEADS, HEAD_PAD), F32).at[:, :, :NOPE_DIM].set(ukv[:, :, :NOPE_DIM])
    uv = ukv[:, :, NOPE_DIM:]
    uv_pad = jnp.zeros((kv_rank, N_HEADS, HEAD_PAD), F32)
    uv_pad = uv_pad.at[:, 0::2, :V_DIM].set(uv[:, 0::2]).at[:, 1::2, V_DIM:2 * V_DIM].set(uv[:, 1::2])
    hp = N_HEADS * HEAD_PAD
    row = lambda v: v.reshape(1, -1).astype(F32)
    return {
        "wq": w_in[:, :off_kv].astype(BF16),
        "wkvc": w_in[:, off_kv:off_kr].astype(BF16),
        "wkr": place_rope(wkr32).astype(BF16),
        "wkrs": place_rope(wkr32[:, partner]).astype(BF16),
        "wca": w_in[:, off_conv:off_conv + d].astype(BF16),
        "wcg": w_in[:, off_conv + d:off_gate].astype(BF16),
        "wgate": w_in[:, off_gate:].astype(BF16),
        "qg": row(q_norm_g),
        "wuq": uq_pad.reshape(q_rank, hp).astype(BF16),
        "wuqs": uq_sw.reshape(q_rank, hp).astype(BF16),
        "kvg": row(kv_norm_g),
        "wukv": jnp.concatenate([uk_pad.reshape(kv_rank, hp), uv_pad.reshape(kv_rank, hp)], axis=1).astype(BF16),
        "womla": w_o_mla.astype(BF16),
        "wpw": w_pw.astype(BF16),
        "wout": w_out.astype(BF16),
        "cw": conv_w.astype(F32),
        "cb": row(conv_b),
        "lng": row(conv_ln_g),
        "lnb": row(conv_ln_b),
    }


def kernel(x, c, ctx, c_ctx, w_mod, b_mod, norm1_g, norm2_g, w_in, q_norm_g, w_uq, kv_norm_g, w_ukv,
           w_o_mla, conv_w, conv_b, conv_ln_g, conv_ln_b, w_pw, w_out, w_pq, sub_keys, u_experts,
           v_experts, final_g):
    assert w_mod.shape[0] == 1, "single-layer configuration"
    b, l, d = x.shape
    assert c.shape[0] + 1 <= 8
    row = lambda v: v.reshape(1, -1).astype(F32)

    cvec = jnp.zeros((8, d), F32).at[:b].set(c).at[b].set(c_ctx)
    mod = _mod_call(cvec, w_mod[0], b_mod[0].reshape(1, -1))
    mod3 = mod.reshape(8, 1, 6 * d)

    wts = _prep_weights(w_in[0], q_norm_g[0], w_uq[0], kv_norm_g[0], w_ukv[0], w_o_mla[0], conv_w[0],
                        conv_b[0], conv_ln_g[0], conv_ln_b[0], w_pw[0], w_out[0])
    tabs = _rope_tables(l)
    g1 = row(norm1_g[0])

    tm = min(256, l)
    q, kl, vl, y, gate = _inproj_call(x, mod3, g1, wts, tabs, tm)
    kc, vc = _ctxkv_call(ctx, mod3, b, g1, wts)
    attn = _attn_call(q, kl, vl, kc, vc, tq=min(256, l), tk=min(512, l))
    x1 = _merge_call(x, mod3, attn, y, gate, wts, tm=min(256, l))

    n_keys = sub_keys.shape[3]
    keys = sub_keys[0].reshape(2 * P_HEADS, n_keys, -1).astype(BF16)
    wpqt = w_pq[0].T.astype(BF16)
    u_b = u_experts[0].astype(BF16)
    vt_b = v_experts[0].T.astype(BF16)
    out = _peer_call(x1.reshape(b * l, d), mod3, row(norm2_g[0]), wpqt, keys, u_b, vt_b, row(final_g),
                     tokens_per_batch=l, tt=min(512, l), et=512)
    return out.reshape(b, l, d)
```

```python
import functools

import jax
import jax.numpy as jnp
import numpy as np
from jax import lax
from jax.experimental import pallas as pl
from jax.experimental.pallas import tpu as pltpu

F32 = jnp.float32
BF16 = jnp.bfloat16

EPS = 1e-6
GRID_W = 64
N_HEADS = 16
NOPE_DIM = 64
ROPE_DIM = 32
V_DIM = 64
ROPE_THETA = 10000.0
ATTN_SCALE = (NOPE_DIM + ROPE_DIM) ** -0.5
LOG2_E = 1.4426950408889634
CONV_WIDTH = 31
CONV_HALF = CONV_WIDTH // 2
P_HEADS = 8
P_TOPK = 16

LANES = 128
SUBLANES = 8
HEAD_PAD = LANES
HALO = 16
VMEM_LIMIT = 56 * 1024 * 1024

_NT = (((1,), (1,)), ((), ()))


def _dot(a, b):
    return jnp.dot(a, b, preferred_element_type=F32)


def _dot_nt(a, b):
    return lax.dot_general(a, b, _NT, preferred_element_type=F32)


def _rms(x):
    return x * lax.rsqrt(jnp.mean(x * x, axis=-1, keepdims=True) + EPS)


def _cparams(sem):
    return pltpu.CompilerParams(dimension_semantics=sem, vmem_limit_bytes=VMEM_LIMIT)


def _mod_kernel(c_ref, w_ref, b_ref, o_ref):
    cv = c_ref[...]
    s = cv * jax.nn.sigmoid(cv)
    o_ref[...] = jnp.dot(s, w_ref[...], preferred_element_type=F32,
                         precision=lax.Precision.HIGHEST) + b_ref[...]


def _mod_call(cvec, w_mod, b_mod):
    rows, d = cvec.shape
    n = w_mod.shape[1]
    tn = d
    return pl.pallas_call(
        _mod_kernel,
        grid=(n // tn,),
        in_specs=[pl.BlockSpec((rows, d), lambda j: (0, 0)),
                  pl.BlockSpec((d, tn), lambda j: (0, j)),
                  pl.BlockSpec((1, tn), lambda j: (0, j))],
        out_specs=pl.BlockSpec((rows, tn), lambda j: (0, j)),
        out_shape=jax.ShapeDtypeStruct((rows, n), F32),
        compiler_params=_cparams(("arbitrary",)),
        name="mod",
    )(cvec, w_mod, b_mod)


def _modulated_norm(x, g, shift, scale):
    return (_rms(x) * g) * (1.0 + scale) + shift


def _kv_from_h(hb, wkvc_ref, kvg_ref, wukv_ref):
    ckv = _dot(hb, wkvc_ref[...])
    ckvn = (_rms(ckv) * kvg_ref[...]).astype(BF16)
    return _dot(ckvn, wukv_ref[...])


def _inproj_kernel(x_ref, mod_ref, g1_ref, wq_ref, wkvc_ref, wkr_ref, wkrs_ref, wca_ref, wcg_ref,
                   wgate_ref, qg_ref, wuq_ref, wuqs_ref, kvg_ref, wukv_ref,
                   cq_ref, sq_ref, ck_ref, sk_ref,
                   q_out, k_out, v_out, y_out, gate_out):
    d = x_ref.shape[-1]
    hp = N_HEADS * HEAD_PAD
    m = mod_ref[0]
    hb = _modulated_norm(x_ref[0], g1_ref[...], m[:, 0:d], m[:, d:2 * d]).astype(BF16)

    cqn = (_rms(_dot(hb, wq_ref[...])) * qg_ref[...]).astype(BF16)
    qa = _dot(cqn, wuq_ref[...])
    qb = _dot(cqn, wuqs_ref[...])
    cq, sq = cq_ref[...], sq_ref[...]
    for h in range(N_HEADS):
        sl = slice(h * HEAD_PAD, (h + 1) * HEAD_PAD)
        q_out[0, :, sl] = (qa[:, sl] * cq + qb[:, sl] * sq).astype(BF16)

    kv = _kv_from_h(hb, wkvc_ref, kvg_ref, wukv_ref)
    kr = _dot(hb, wkr_ref[...]) * ck_ref[...] + _dot(hb, wkrs_ref[...]) * sk_ref[...]
    for h in range(N_HEADS):
        sl = slice(h * HEAD_PAD, (h + 1) * HEAD_PAD)
        k_out[0, :, sl] = (kv[:, sl] + kr).astype(BF16)
    v_out[0] = kv[:, hp:].astype(BF16)

    a = _dot(hb, wca_ref[...])
    g = _dot(hb, wcg_ref[...])
    y_out[0] = (a * jax.nn.sigmoid(g)).astype(BF16)
    gate_out[0] = jax.nn.sigmoid(_dot(hb, wgate_ref[...])).astype(BF16)


def _const_spec(arr):
    nd = arr.ndim
    return pl.BlockSpec(arr.shape, lambda *_: (0,) * nd)


def _inproj_call(x, mod3, g1, wts, tabs, tm):
    b, l, d = x.shape
    hp = N_HEADS * HEAD_PAD
    nt = l // tm
    weights = [wts[k] for k in ("wq", "wkvc", "wkr", "wkrs", "wca", "wcg", "wgate",
                                "qg", "wuq", "wuqs", "kvg", "wukv")]
    tok = lambda w: pl.BlockSpec((1, tm, w), lambda bi, i: (bi, i, 0))
    tab = pl.BlockSpec((tm, HEAD_PAD), lambda bi, i: (i, 0))
    return pl.pallas_call(
        _inproj_kernel,
        grid=(b, nt),
        in_specs=[tok(d),
                  pl.BlockSpec((1, 1, mod3.shape[-1]), lambda bi, i: (bi, 0, 0)),
                  _const_spec(g1)] + [_const_spec(w) for w in weights] + [tab] * 4,
        out_specs=[tok(hp), tok(hp), tok(hp), tok(d), tok(2 * d)],
        out_shape=[jax.ShapeDtypeStruct((b, l, hp), BF16),
                   jax.ShapeDtypeStruct((b, l, hp), BF16),
                   jax.ShapeDtypeStruct((b, l, hp), BF16),
                   jax.ShapeDtypeStruct((b, l, d), BF16),
                   jax.ShapeDtypeStruct((b, l, 2 * d), BF16)],
        compiler_params=_cparams(("parallel", "parallel")),
        name="in_proj",
    )(x, mod3, g1, *weights, tabs["cq"], tabs["sq"], tabs["ck"], tabs["sk"])


def _ctxkv_kernel(x_ref, mod_ref, g1_ref, wkvc_ref, wkr_ref, kvg_ref, wukv_ref, k_out, v_out):
    d = x_ref.shape[-1]
    hp = N_HEADS * HEAD_PAD
    m = mod_ref[0]
    hb = _modulated_norm(x_ref[0], g1_ref[...], m[:, 0:d], m[:, d:2 * d]).astype(BF16)
    kv = _kv_from_h(hb, wkvc_ref, kvg_ref, wukv_ref)
    kr = _dot(hb, wkr_ref[...])
    for h in range(N_HEADS):
        sl = slice(h * HEAD_PAD, (h + 1) * HEAD_PAD)
        k_out[0, :, sl] = (kv[:, sl] + kr).astype(BF16)
    v_out[0] = kv[:, hp:].astype(BF16)


def _ctxkv_call(ctx, mod3, ctx_row, g1, wts):
    b, n, d = ctx.shape
    hp = N_HEADS * HEAD_PAD
    weights = [wts[k] for k in ("wkvc", "wkr", "kvg", "wukv")]
    tok = lambda w: pl.BlockSpec((1, n, w), lambda bi: (bi, 0, 0))
    return pl.pallas_call(
        _ctxkv_kernel,
        grid=(b,),
        in_specs=[tok(d),
                  pl.BlockSpec((1, 1, mod3.shape[-1]), lambda bi: (ctx_row, 0, 0)),
                  _const_spec(g1)] + [_const_spec(w) for w in weights],
        out_specs=[tok(hp), tok(hp)],
        out_shape=[jax.ShapeDtypeStruct((b, n, hp), BF16)] * 2,
        compiler_params=_cparams(("parallel",)),
        name="ctx_kv",
    )(ctx, mod3, g1, *weights)


def _attn_kernel(q_ref, kl_ref, vl_ref, kc_ref, vc_ref, o_ref, m_ref, l_ref, acc_ref, *, tk):
    n_chunks = kl_ref.shape[1] // tk
    m_ref[...] = jnp.full(m_ref.shape, -jnp.inf, F32)
    l_ref[...] = jnp.zeros(l_ref.shape, F32)
    acc_ref[...] = jnp.zeros(acc_ref.shape, F32)

    def update(head, k, v):
        sl = slice(head * HEAD_PAD, (head + 1) * HEAD_PAD)
        s = _dot_nt(q_ref[0, :, sl], k)
        tiles = [s[:, t * LANES:(t + 1) * LANES] for t in range(s.shape[1] // LANES)]
        tile_max = functools.reduce(jnp.maximum, tiles)
        m_old = m_ref[head]
        m_new = jnp.maximum(m_old, jnp.max(tile_max, axis=-1, keepdims=True))
        alpha = jnp.exp2(m_old - m_new)
        ps = [jnp.exp2(t - m_new) for t in tiles]
        l_ref[head] = alpha * l_ref[head] + functools.reduce(jnp.add, ps)
        p = jnp.concatenate([t.astype(BF16) for t in ps], axis=1)
        acc_ref[head] = alpha * acc_ref[head] + _dot(p, v)
        m_ref[head] = m_new

    def body(c, _):
        off = pl.multiple_of(c * tk, tk)
        for head in range(2):
            sl = slice(head * HEAD_PAD, (head + 1) * HEAD_PAD)
            update(head, kl_ref[0, pl.ds(off, tk), sl], vl_ref[0, pl.ds(off, tk), sl])
        return 0

    lax.fori_loop(0, n_chunks, body, 0)
    out = None
    for head in range(2):
        sl = slice(head * HEAD_PAD, (head + 1) * HEAD_PAD)
        update(head, kc_ref[0, :, sl], vc_ref[0, :, sl])
        o = acc_ref[head] * (1.0 / jnp.sum(l_ref[head], axis=-1, keepdims=True))
        out = o if out is None else out + o
    o_ref[0] = out.astype(BF16)


def _attn_call(q, kl, vl, kc, vc, tq, tk):
    b, l, hp = q.shape
    nc = kc.shape[1]
    pairs = N_HEADS // 2
    pw = 2 * HEAD_PAD
    return pl.pallas_call(
        functools.partial(_attn_kernel, tk=tk),
        grid=(b, pairs, l // tq),
        in_specs=[pl.BlockSpec((1, tq, pw), lambda bi, p, i: (bi, i, p)),
                  pl.BlockSpec((1, l, pw), lambda bi, p, i: (bi, 0, p)),
                  pl.BlockSpec((1, l, pw), lambda bi, p, i: (bi, 0, p)),
                  pl.BlockSpec((1, nc, pw), lambda bi, p, i: (bi, 0, p)),
                  pl.BlockSpec((1, nc, pw), lambda bi, p, i: (bi, 0, p))],
        out_specs=pl.BlockSpec((1, tq, HEAD_PAD), lambda bi, p, i: (bi, i, p)),
        out_shape=jax.ShapeDtypeStruct((b, l, pairs * HEAD_PAD), BF16),
        scratch_shapes=[pltpu.VMEM((2, tq, HEAD_PAD), F32)] * 3,
        compiler_params=_cparams(("parallel", "parallel", "parallel")),
        name="attention",
    )(q, kl, vl, kc, vc)


def _merge_kernel(x_ref, mod_ref, attn_ref, y_ref, yp_ref, yn_ref, gate_ref,
                  womla_ref, wpw_ref, wout_ref, cw_ref, cb_ref, lng_ref, lnb_ref,
                  o_ref, ybuf, sbuf, cbuf, *, rows):
    i = pl.program_id(1)
    last = pl.num_programs(1) - 1
    tm, d = y_ref.shape[1], y_ref.shape[2]

    prev = yp_ref[0].astype(F32)
    nxt = yn_ref[0].astype(F32)
    ybuf[0:HALO, :] = jnp.where(i > 0, prev, 0.0)
    ybuf[HALO:HALO + tm, :] = y_ref[0].astype(F32)
    ybuf[HALO + tm:2 * HALO + tm, :] = jnp.where(i < last, nxt, 0.0)
    span = sbuf.shape[1]
    for p in range(SUBLANES):
        sbuf[p] = ybuf[p:p + span, :]

    cb = cb_ref[...]

    def conv_rows(r, _):
        base = pl.multiple_of(r * rows, rows)
        acc = jnp.broadcast_to(cb, (rows, d))
        for k in range(CONV_WIDTH):
            q, p = divmod(HALO - CONV_HALF + k, SUBLANES)
            acc = acc + cw_ref[k:k + 1, :] * sbuf[p, pl.ds(base + q * SUBLANES, rows), :]
        cbuf[pl.ds(base, rows), :] = acc
        return 0

    lax.fori_loop(0, tm // rows, conv_rows, 0)

    yc = cbuf[...]
    mu = jnp.mean(yc, axis=-1, keepdims=True)
    cen = yc - mu
    var = jnp.mean(cen * cen, axis=-1, keepdims=True)
    ln = cen * lax.rsqrt(var + EPS) * lng_ref[...] + lnb_ref[...]
    act = (ln * jax.nn.sigmoid(ln)).astype(BF16)
    y_c = _dot(act, wpw_ref[...])
    y_a = _dot(attn_ref[0], womla_ref[...])
    gate = gate_ref[0]
    z = gate[:, :d].astype(F32) * y_a + gate[:, d:].astype(F32) * y_c
    zo = _dot(z.astype(BF16), wout_ref[...])
    m = mod_ref[0]
    o_ref[0] = x_ref[0] + m[:, 2 * d:3 * d] * zo


def _merge_call(x, mod3, attn, y, gate, wts, tm):
    b, l, d = x.shape
    nt = l // tm
    hb = tm // HALO
    nhb = l // HALO
    weights = [wts[k] for k in ("womla", "wpw", "wout", "cw", "cb", "lng", "lnb")]
    tok = lambda w: pl.BlockSpec((1, tm, w), lambda bi, i: (bi, i, 0))
    return pl.pallas_call(
        functools.partial(_merge_kernel, rows=32),
        grid=(b, nt),
        in_specs=[tok(d),
                  pl.BlockSpec((1, 1, mod3.shape[-1]), lambda bi, i: (bi, 0, 0)),
                  tok(d), tok(d),
                  pl.BlockSpec((1, HALO, d), lambda bi, i: (bi, jnp.maximum(i * hb - 1, 0), 0)),
                  pl.BlockSpec((1, HALO, d), lambda bi, i: (bi, jnp.minimum((i + 1) * hb, nhb - 1), 0)),
                  tok(2 * d)] + [_const_spec(w) for w in weights],
        out_specs=tok(d),
        out_shape=jax.ShapeDtypeStruct((b, l, d), F32),
        scratch_shapes=[pltpu.VMEM((tm + 2 * HALO, d), F32),
                        pltpu.VMEM((SUBLANES, tm + 2 * HALO - SUBLANES, d), F32),
                        pltpu.VMEM((tm, d), F32)],
        compiler_params=_cparams(("parallel", "parallel")),
        name="merge",
    )(x, mod3, attn, y, y, y, gate, *weights)


def _topk_rank(s):
    nk = s.shape[0]
    key_id = lax.broadcasted_iota(jnp.int32, s.shape, 0)
    rank = jnp.full(s.shape, float(P_TOPK), F32)
    top_id = lax.broadcasted_iota(jnp.int32, (P_TOPK, s.shape[1]), 0)
    tops = jnp.zeros((P_TOPK, s.shape[1]), F32)
    for r in range(P_TOPK):
        cur = jnp.max(s, axis=0, keepdims=True)
        first = jnp.min(jnp.where(s == cur, key_id, nk), axis=0, keepdims=True)
        hit = key_id == first
        rank = jnp.where(hit, float(r), rank)
        s = jnp.where(hit, -jnp.inf, s)
        tops = jnp.where(top_id == r, cur, tops)
    return tops, rank


def _staircase(a, b):
    row_id = lax.broadcasted_iota(jnp.int32, a.shape, 0)
    cnt = jnp.zeros(a.shape, F32)
    front = a + b[0:1]
    for _ in range(P_TOPK):
        best = jnp.max(front, axis=0, keepdims=True)
        pick = jnp.min(jnp.where(front == best, row_id, P_TOPK), axis=0, keepdims=True)
        hit = row_id == pick
        cnt = jnp.where(hit, cnt + 1.0, cnt)
        nxt = jnp.full(a.shape, -jnp.inf, F32)
        for k in range(1, P_TOPK):
            nxt = jnp.where(cnt == float(k), b[k:k + 1], nxt)
        front = jnp.where(hit, a + nxt, front)
    return cnt


def _peer_kernel(x_ref, mod_ref, g2_ref, wpqt_ref, keys_ref, u_ref, vt_ref, fg_ref, o_ref,
                 h2_ref, sa_ref, rc_ref, top_ref, at_ref, gt_ref, acc_ref):
    j = pl.program_id(1)
    n_et = pl.num_programs(1)
    tt, d = x_ref.shape
    nlt = tt // LANES
    nk = keys_ref.shape[1]
    et = u_ref.shape[0]
    ke = et // nk
    m = mod_ref[0]

    @pl.when(j == 0)
    def _prepare():
        h2 = _modulated_norm(x_ref[...], g2_ref[...], m[:, 3 * d:4 * d], m[:, 4 * d:5 * d]).astype(BF16)
        h2_ref[...] = h2
        dk = keys_ref.shape[2]
        qt = _dot_nt(wpqt_ref[...], h2).astype(BF16)
        for hp in range(2 * P_HEADS):
            s = _dot(keys_ref[hp], qt[hp * dk:(hp + 1) * dk, :])
            for lt in range(nlt):
                sa_ref[hp, lt] = s[:, lt * LANES:(lt + 1) * LANES]

        def rank_unit(u, _):
            hp, lt = u // nlt, u % nlt
            tops, rank = _topk_rank(sa_ref[hp, lt])
            top_ref[hp, lt] = tops
            rc_ref[hp, lt] = rank
            return 0

        lax.fori_loop(0, 2 * P_HEADS * nlt, rank_unit, 0)

        def gate_unit(u, _):
            h, lt = u // nlt, u % nlt
            a, b = top_ref[2 * h, lt], top_ref[2 * h + 1, lt]
            cnt = _staircase(a, b)
            ea = jnp.exp(a - a[0:1])
            eb = jnp.exp(b - b[0:1])
            cum = jnp.zeros(a.shape, F32)
            for k in range(P_TOPK):
                cum = cum + jnp.where(cnt > float(k), eb[k:k + 1], 0.0)
            inv_z = 1.0 / jnp.sum(ea * cum, axis=0, keepdims=True)
            r1 = rc_ref[2 * h, lt]
            cnt1 = jnp.zeros(r1.shape, F32)
            for k in range(P_TOPK):
                cnt1 = jnp.where(r1 == float(k), cnt[k:k + 1], cnt1)
            rc_ref[2 * h, lt] = cnt1
            sa_ref[2 * h, lt] = jnp.exp(sa_ref[2 * h, lt] - a[0:1]) * inv_z
            sa_ref[2 * h + 1, lt] = jnp.exp(sa_ref[2 * h + 1, lt] - b[0:1])
            return 0

        lax.fori_loop(0, P_HEADS * nlt, gate_unit, 0)
        acc_ref[...] = jnp.zeros(acc_ref.shape, F32)

    at_ref[...] = _dot_nt(u_ref[...], h2_ref[...])

    def weight_rows(r, _):
        e1 = j * ke + r
        base = pl.multiple_of(r * nk, nk)
        for lt in range(nlt):
            ls = slice(lt * LANES, (lt + 1) * LANES)
            w = jnp.zeros((nk, LANES), F32)
            for h in range(P_HEADS):
                cnt_row = rc_ref[2 * h, lt, pl.ds(e1, 1), :]
                a_row = sa_ref[2 * h, lt, pl.ds(e1, 1), :]
                w = w + jnp.where(rc_ref[2 * h + 1, lt] < cnt_row, a_row * sa_ref[2 * h + 1, lt], 0.0)
            pre = at_ref[pl.ds(base, nk), ls]
            act = 0.5 * pre * (1.0 + lax.erf(pre * np.float32(2.0 ** -0.5)))
            gt_ref[pl.ds(base, nk), ls] = (act * w).astype(BF16)
        return 0

    lax.fori_loop(0, ke, weight_rows, 0)
    acc_ref[...] += _dot(vt_ref[...], gt_ref[...])

    @pl.when(j == n_et - 1)
    def _finish():
        x2 = x_ref[...] + m[:, 5 * d:6 * d] * acc_ref[...].T
        o_ref[...] = _rms(x2) * fg_ref[...]


def _peer_call(x1, mod3, g2, wpqt, keys, u_b, vt_b, fg, tokens_per_batch, tt, et):
    n, d = x1.shape
    ne = u_b.shape[0]
    nk = keys.shape[1]
    nlt = tt // LANES
    tiles_per_batch = tokens_per_batch // tt
    unit = (2 * P_HEADS, nlt, nk, LANES)
    return pl.pallas_call(
        _peer_kernel,
        grid=(n // tt, ne // et),
        in_specs=[pl.BlockSpec((tt, d), lambda t, j: (t, 0)),
                  pl.BlockSpec((1, 1, mod3.shape[-1]), lambda t, j: (t // tiles_per_batch, 0, 0)),
                  _const_spec(g2), _const_spec(wpqt), _const_spec(keys),
                  pl.BlockSpec((et, d), lambda t, j: (j, 0)),
                  pl.BlockSpec((d, et), lambda t, j: (0, j)),
                  _const_spec(fg)],
        out_specs=pl.BlockSpec((tt, d), lambda t, j: (t, 0)),
        out_shape=jax.ShapeDtypeStruct((n, d), F32),
        scratch_shapes=[pltpu.VMEM((tt, d), BF16),
                        pltpu.VMEM(unit, F32),
                        pltpu.VMEM(unit, F32),
                        pltpu.VMEM((2 * P_HEADS, nlt, P_TOPK, LANES), F32),
                        pltpu.VMEM((et, tt), F32),
                        pltpu.VMEM((et, tt), BF16),
                        pltpu.VMEM((d, tt), F32)],
        compiler_params=_cparams(("parallel", "arbitrary")),
        name="peer",
    )(x1, mod3, g2, wpqt, keys, u_b, vt_b, fg)


def _rope_tables(n_tok):
    rows = n_tok // GRID_W
    r = jnp.repeat(jnp.arange(rows, dtype=jnp.int32), GRID_W)
    col = jnp.tile(jnp.arange(GRID_W, dtype=jnp.int32), rows)
    n_freq = ROPE_DIM // 4
    freq = ROPE_THETA ** (-jnp.arange(n_freq, dtype=F32) / n_freq)
    ang = jnp.stack([r[:, None] * freq, col[:, None] * freq], axis=1)
    dd = np.arange(ROPE_DIM)
    axis, half, fr = dd // (ROPE_DIM // 2), (dd % (ROPE_DIM // 2)) // n_freq, dd % n_freq
    c32 = jnp.cos(ang)[:, axis, fr]
    s32 = jnp.sin(ang)[:, axis, fr] * jnp.where(half == 0, -1.0, 1.0).astype(F32)
    pad = jnp.zeros((n_tok, HEAD_PAD - NOPE_DIM - ROPE_DIM), F32)
    ones = jnp.ones((n_tok, NOPE_DIM), F32)
    ck = jnp.concatenate([ones, c32, pad], axis=1)
    sk = jnp.concatenate([0.0 * ones, s32, pad], axis=1)
    q_scale = ATTN_SCALE * LOG2_E
    return {"cq": ck * q_scale, "sq": sk * q_scale, "ck": ck, "sk": sk}


def _prep_weights(w_in, q_norm_g, w_uq, kv_norm_g, w_ukv, w_o_mla, conv_w, conv_b, conv_ln_g,
                  conv_ln_b, w_pw, w_out):
    d = w_in.shape[0]
    q_rank = q_norm_g.shape[0]
    kv_rank = kv_norm_g.shape[0]
    off_kv = q_rank
    off_kr = q_rank + kv_rank
    off_conv = off_kr + ROPE_DIM
    off_gate = off_conv + 2 * d
    qk = NOPE_DIM + ROPE_DIM
    tail = HEAD_PAD - qk

    def lane_pad(w, before, after):
        return jnp.pad(w, [(0, 0)] * (w.ndim - 1) + [(before, after)])

    def partner(w):
        g = w.reshape(w.shape[:-1] + (2, 2, ROPE_DIM // 4))
        return g[..., ::-1, :].reshape(w.shape)

    def place_rope(w32):
        return lane_pad(w32, NOPE_DIM, tail)

    wkr32 = w_in[:, off_kr:off_conv]
    uq = w_uq.reshape(q_rank, N_HEADS, qk)
    uq_pad = lane_pad(uq, 0, tail)
    uq_sw = lane_pad(partner(uq[:, :, NOPE_DIM:]), NOPE_DIM, tail)
    ukv = w_ukv.reshape(kv_rank, N_HEADS, NOPE_DIM + V_DIM)
    uk_pad = lane_pad(ukv[:, :, :NOPE_DIM], 0, HEAD_PAD - NOPE_DIM)
    uv = ukv[:, :, NOPE_DIM:]
    even = (np.arange(N_HEADS) % 2 == 0)[None, :, None]
    uv_pad = jnp.concatenate([jnp.where(even, uv, 0.0), jnp.where(even, 0.0, uv)], axis=-1)
    hp = N_HEADS * HEAD_PAD
    row = lambda v: v.reshape(1, -1).astype(F32)
    return {
        "wq": w_in[:, :off_kv].astype(BF16),
        "wkvc": w_in[:, off_kv:off_kr].astype(BF16),
        "wkr": place_rope(wkr32).astype(BF16),
        "wkrs": place_rope(partner(wkr32)).astype(BF16),
        "wca": w_in[:, off_conv:off_conv + d].astype(BF16),
        "wcg": w_in[:, off_conv + d:off_gate].astype(BF16),
        "wgate": w_in[:, off_gate:].astype(BF16),
        "qg": row(q_norm_g),
        "wuq": uq_pad.reshape(q_rank, hp).astype(BF16),
        "wuqs": uq_sw.reshape(q_rank, hp).astype(BF16),
        "kvg": row(kv_norm_g),
        "wukv": jnp.concatenate([uk_pad.reshape(kv_rank, hp), uv_pad.reshape(kv_rank, hp)], axis=1).astype(BF16),
        "womla": w_o_mla.astype(BF16),
        "wpw": w_pw.astype(BF16),
        "wout": w_out.astype(BF16),
        "cw": conv_w.astype(F32),
        "cb": row(conv_b),
        "lng": row(conv_ln_g),
        "lnb": row(conv_ln_b),
    }


def kernel(x, c, ctx, c_ctx, w_mod, b_mod, norm1_g, norm2_g, w_in, q_norm_g, w_uq, kv_norm_g, w_ukv,
           w_o_mla, conv_w, conv_b, conv_ln_g, conv_ln_b, w_pw, w_out, w_pq, sub_keys, u_experts,
           v_experts, final_g):
    assert w_mod.shape[0] == 1, "single-layer configuration"
    b, l, d = x.shape
    assert c.shape[0] + 1 <= 8
    row = lambda v: v.reshape(1, -1).astype(F32)

    cvec = jnp.concatenate([c, c_ctx[None, :], jnp.zeros((SUBLANES - b - 1, d), F32)], axis=0)
    mod = _mod_call(cvec, w_mod[0], b_mod[0].reshape(1, -1))
    mod3 = mod.reshape(8, 1, 6 * d)

    wts = _prep_weights(w_in[0], q_norm_g[0], w_uq[0], kv_norm_g[0], w_ukv[0], w_o_mla[0], conv_w[0],
                        conv_b[0], conv_ln_g[0], conv_ln_b[0], w_pw[0], w_out[0])
    tabs = _rope_tables(l)
    g1 = row(norm1_g[0])

    tm = min(256, l)
    q, kl, vl, y, gate = _inproj_call(x, mod3, g1, wts, tabs, tm)
    kc, vc = _ctxkv_call(ctx, mod3, b, g1, wts)
    attn = _attn_call(q, kl, vl, kc, vc, tq=min(512, l), tk=min(512, l))
    x1 = _merge_call(x, mod3, attn, y, gate, wts, tm=min(256, l))

    n_keys = sub_keys.shape[3]
    keys = sub_keys[0].reshape(2 * P_HEADS, n_keys, -1).astype(BF16)
    wpqt = w_pq[0].T.astype(BF16)
    u_b = u_experts[0].astype(BF16)
    vt_b = v_experts[0].T.astype(BF16)
    out = _peer_call(x1.reshape(b * l, d), mod3, row(norm2_g[0]), wpqt, keys, u_b, vt_b, row(final_g),
                     tokens_per_batch=l, tt=min(512, l), et=512)
    return out.reshape(b, l, d)
```

```python
import functools

import jax
import jax.numpy as jnp
import numpy as np
from jax import lax
from jax.experimental import pallas as pl
from jax.experimental.pallas import tpu as pltpu

F32 = jnp.float32
BF16 = jnp.bfloat16

EPS = 1e-6
GRID_W = 64
N_HEADS = 16
NOPE_DIM = 64
ROPE_DIM = 32
V_DIM = 64
ROPE_THETA = 10000.0
ATTN_SCALE = (NOPE_DIM + ROPE_DIM) ** -0.5
LOG2_E = 1.4426950408889634
CONV_WIDTH = 31
CONV_HALF = CONV_WIDTH // 2
P_HEADS = 8
P_TOPK = 16

LANES = 128
SUBLANES = 8
BF16_ROWS = 16
HEAD_PAD = LANES
HALO = 16
MM_ROWS = 256
VMEM_LIMIT = 56 * 1024 * 1024

_NT = (((1,), (1,)), ((), ()))


def _dot(a, b):
    return jnp.dot(a, b, preferred_element_type=F32)


def _dot_nt(a, b):
    return lax.dot_general(a, b, _NT, preferred_element_type=F32)


def _rms(x):
    return x * lax.rsqrt(jnp.mean(x * x, axis=-1, keepdims=True) + EPS)


def _cparams(sem):
    return pltpu.CompilerParams(dimension_semantics=sem, vmem_limit_bytes=VMEM_LIMIT)


def _mod_kernel(c_ref, w_ref, b_ref, o_ref):
    cv = c_ref[...]
    s = cv * jax.nn.sigmoid(cv)
    o_ref[...] = jnp.dot(s, w_ref[...], preferred_element_type=F32,
                         precision=lax.Precision.HIGHEST) + b_ref[...]


def _mod_call(cvec, w_mod, b_mod):
    rows, d = cvec.shape
    n = w_mod.shape[1]
    tn = d
    return pl.pallas_call(
        _mod_kernel,
        grid=(n // tn,),
        in_specs=[pl.BlockSpec((rows, d), lambda j: (0, 0)),
                  pl.BlockSpec((d, tn), lambda j: (0, j)),
                  pl.BlockSpec((1, tn), lambda j: (0, j))],
        out_specs=pl.BlockSpec((rows, tn), lambda j: (0, j)),
        out_shape=jax.ShapeDtypeStruct((rows, n), F32),
        compiler_params=_cparams(("arbitrary",)),
        name="mod",
    )(cvec, w_mod, b_mod)


def _modulated_norm(x, g, shift, scale):
    return (_rms(x) * g) * (1.0 + scale) + shift


def _kv_from_h(hb, wkvc_ref, kvg_ref, wukv_ref):
    ckv = _dot(hb, wkvc_ref[...])
    ckvn = (_rms(ckv) * kvg_ref[...]).astype(BF16)
    return _dot(ckvn, wukv_ref[...])


def _inproj_kernel(x_ref, mod_ref, g1_ref, wq_ref, wkvc_ref, wkr_ref, wkrs_ref, wca_ref, wcg_ref,
                   wgate_ref, qg_ref, wuq_ref, wuqs_ref, kvg_ref, wukv_ref,
                   cq_ref, sq_ref, ck_ref, sk_ref,
                   q_out, k_out, v_out, y_out, gate_out):
    d = x_ref.shape[-1]
    hp = N_HEADS * HEAD_PAD
    m = mod_ref[0]
    hb = _modulated_norm(x_ref[0], g1_ref[...], m[:, 0:d], m[:, d:2 * d]).astype(BF16)

    cqn = (_rms(_dot(hb, wq_ref[...])) * qg_ref[...]).astype(BF16)
    qa = _dot(cqn, wuq_ref[...])
    qb = _dot(cqn, wuqs_ref[...])
    cq, sq = cq_ref[...], sq_ref[...]
    for h in range(N_HEADS):
        sl = slice(h * HEAD_PAD, (h + 1) * HEAD_PAD)
        q_out[0, :, sl] = (qa[:, sl] * cq + qb[:, sl] * sq).astype(BF16)

    kv = _kv_from_h(hb, wkvc_ref, kvg_ref, wukv_ref)
    kr = _dot(hb, wkr_ref[...]) * ck_ref[...] + _dot(hb, wkrs_ref[...]) * sk_ref[...]
    for h in range(N_HEADS):
        sl = slice(h * HEAD_PAD, (h + 1) * HEAD_PAD)
        k_out[0, :, sl] = (kv[:, sl] + kr).astype(BF16)
    v_out[0] = kv[:, hp:].astype(BF16)

    a = _dot(hb, wca_ref[...])
    g = _dot(hb, wcg_ref[...])
    y_out[0] = (a * jax.nn.sigmoid(g)).astype(BF16)
    gate_out[0] = jax.nn.sigmoid(_dot(hb, wgate_ref[...])).astype(BF16)


def _const_spec(arr):
    nd = arr.ndim
    return pl.BlockSpec(arr.shape, lambda *_: (0,) * nd)


def _inproj_call(x, mod3, g1, wts, tabs, tm):
    b, l, d = x.shape
    hp = N_HEADS * HEAD_PAD
    nt = l // tm
    weights = [wts[k] for k in ("wq", "wkvc", "wkr", "wkrs", "wca", "wcg", "wgate",
                                "qg", "wuq", "wuqs", "kvg", "wukv")]
    tok = lambda w: pl.BlockSpec((1, tm, w), lambda bi, i: (bi, i, 0))
    tab = pl.BlockSpec((tm, HEAD_PAD), lambda bi, i: (i, 0))
    return pl.pallas_call(
        _inproj_kernel,
        grid=(b, nt),
        in_specs=[tok(d),
                  pl.BlockSpec((1, 1, mod3.shape[-1]), lambda bi, i: (bi, 0, 0)),
                  _const_spec(g1)] + [_const_spec(w) for w in weights] + [tab] * 4,
        out_specs=[tok(hp), tok(hp), tok(hp), tok(d), tok(2 * d)],
        out_shape=[jax.ShapeDtypeStruct((b, l, hp), BF16),
                   jax.ShapeDtypeStruct((b, l, hp), BF16),
                   jax.ShapeDtypeStruct((b, l, hp), BF16),
                   jax.ShapeDtypeStruct((b, l, d), BF16),
                   jax.ShapeDtypeStruct((b, l, 2 * d), BF16)],
        compiler_params=_cparams(("parallel", "parallel")),
        name="in_proj",
    )(x, mod3, g1, *weights, tabs["cq"], tabs["sq"], tabs["ck"], tabs["sk"])


def _ctxkv_kernel(x_ref, mod_ref, g1_ref, wkvc_ref, wkr_ref, kvg_ref, wukv_ref, k_out, v_out):
    d = x_ref.shape[-1]
    hp = N_HEADS * HEAD_PAD
    m = mod_ref[0]
    hb = _modulated_norm(x_ref[0], g1_ref[...], m[:, 0:d], m[:, d:2 * d]).astype(BF16)
    kv = _kv_from_h(hb, wkvc_ref, kvg_ref, wukv_ref)
    kr = _dot(hb, wkr_ref[...])
    for h in range(N_HEADS):
        sl = slice(h * HEAD_PAD, (h + 1) * HEAD_PAD)
        k_out[0, :, sl] = (kv[:, sl] + kr).astype(BF16)
    v_out[0] = kv[:, hp:].astype(BF16)


def _ctxkv_call(ctx, mod3, ctx_row, g1, wts):
    b, n, d = ctx.shape
    hp = N_HEADS * HEAD_PAD
    weights = [wts[k] for k in ("wkvc", "wkr", "kvg", "wukv")]
    tok = lambda w: pl.BlockSpec((1, n, w), lambda bi: (bi, 0, 0))
    return pl.pallas_call(
        _ctxkv_kernel,
        grid=(b,),
        in_specs=[tok(d),
                  pl.BlockSpec((1, 1, mod3.shape[-1]), lambda bi: (ctx_row, 0, 0)),
                  _const_spec(g1)] + [_const_spec(w) for w in weights],
        out_specs=[tok(hp), tok(hp)],
        out_shape=[jax.ShapeDtypeStruct((b, n, hp), BF16)] * 2,
        compiler_params=_cparams(("parallel",)),
        name="ctx_kv",
    )(ctx, mod3, g1, *weights)


def _attn_kernel(q_ref, kl_ref, vl_ref, kc_ref, vc_ref, o_ref, m_ref, l_ref, acc_ref, *, tk):
    n_chunks = kl_ref.shape[1] // tk
    m_ref[...] = jnp.full(m_ref.shape, -jnp.inf, F32)
    l_ref[...] = jnp.zeros(l_ref.shape, F32)
    acc_ref[...] = jnp.zeros(acc_ref.shape, F32)

    def update(head, k, v):
        sl = slice(head * HEAD_PAD, (head + 1) * HEAD_PAD)
        s = _dot_nt(q_ref[0, :, sl], k)
        tiles = [s[:, t * LANES:(t + 1) * LANES] for t in range(s.shape[1] // LANES)]
        tile_max = functools.reduce(jnp.maximum, tiles)
        m_old = m_ref[head]
        m_new = jnp.maximum(m_old, jnp.max(tile_max, axis=-1, keepdims=True))
        alpha = jnp.exp2(m_old - m_new)
        ps = [jnp.exp2(t - m_new) for t in tiles]
        l_ref[head] = alpha * l_ref[head] + functools.reduce(jnp.add, ps)
        p = jnp.concatenate([t.astype(BF16) for t in ps], axis=1)
        acc_ref[head] = alpha * acc_ref[head] + _dot(p, v)
        m_ref[head] = m_new

    def body(c, _):
        off = pl.multiple_of(c * tk, tk)
        for head in range(2):
            sl = slice(head * HEAD_PAD, (head + 1) * HEAD_PAD)
            update(head, kl_ref[0, pl.ds(off, tk), sl], vl_ref[0, pl.ds(off, tk), sl])
        return 0

    lax.fori_loop(0, n_chunks, body, 0)
    out = None
    for head in range(2):
        sl = slice(head * HEAD_PAD, (head + 1) * HEAD_PAD)
        update(head, kc_ref[0, :, sl], vc_ref[0, :, sl])
        o = acc_ref[head] * (1.0 / jnp.sum(l_ref[head], axis=-1, keepdims=True))
        out = o if out is None else out + o
    o_ref[0] = out.astype(BF16)


def _attn_call(q, kl, vl, kc, vc, tq, tk):
    b, l, hp = q.shape
    nc = kc.shape[1]
    pairs = N_HEADS // 2
    pw = 2 * HEAD_PAD
    return pl.pallas_call(
        functools.partial(_attn_kernel, tk=tk),
        grid=(b, pairs, l // tq),
        in_specs=[pl.BlockSpec((1, tq, pw), lambda bi, p, i: (bi, i, p)),
                  pl.BlockSpec((1, l, pw), lambda bi, p, i: (bi, 0, p)),
                  pl.BlockSpec((1, l, pw), lambda bi, p, i: (bi, 0, p)),
                  pl.BlockSpec((1, nc, pw), lambda bi, p, i: (bi, 0, p)),
                  pl.BlockSpec((1, nc, pw), lambda bi, p, i: (bi, 0, p))],
        out_specs=pl.BlockSpec((1, tq, HEAD_PAD), lambda bi, p, i: (bi, i, p)),
        out_shape=jax.ShapeDtypeStruct((b, l, pairs * HEAD_PAD), BF16),
        scratch_shapes=[pltpu.VMEM((2, tq, HEAD_PAD), F32)] * 3,
        compiler_params=_cparams(("parallel", "parallel", "parallel")),
        name="attention",
    )(q, kl, vl, kc, vc)


def _merge_kernel(x_ref, mod_ref, attn_ref, y_ref, yp_ref, yn_ref, gate_ref,
                  womla_ref, wpw_ref, wout_ref, cw_ref, cb_ref, lng_ref, lnb_ref,
                  o_ref, ybuf, sbuf, cbuf, *, rows):
    i = pl.program_id(1)
    last = pl.num_programs(1) - 1
    tm, d = y_ref.shape[1], y_ref.shape[2]

    prev = yp_ref[0].astype(F32)
    nxt = yn_ref[0].astype(F32)
    ybuf[0:HALO, :] = jnp.where(i > 0, prev, 0.0)
    ybuf[HALO:HALO + tm, :] = y_ref[0].astype(F32)
    ybuf[HALO + tm:2 * HALO + tm, :] = jnp.where(i < last, nxt, 0.0)
    span = sbuf.shape[1]
    for p in range(SUBLANES):
        sbuf[p] = ybuf[p:p + span, :]

    cb = cb_ref[...]

    def conv_rows(r, _):
        base = pl.multiple_of(r * rows, rows)
        acc = jnp.broadcast_to(cb, (rows, d))
        for k in range(CONV_WIDTH):
            q, p = divmod(HALO - CONV_HALF + k, SUBLANES)
            acc = acc + cw_ref[k:k + 1, :] * sbuf[p, pl.ds(base + q * SUBLANES, rows), :]
        cbuf[pl.ds(base, rows), :] = acc
        return 0

    lax.fori_loop(0, tm // rows, conv_rows, 0)

    yc = cbuf[...]
    mu = jnp.mean(yc, axis=-1, keepdims=True)
    cen = yc - mu
    var = jnp.mean(cen * cen, axis=-1, keepdims=True)
    ln = cen * lax.rsqrt(var + EPS) * lng_ref[...] + lnb_ref[...]
    act = (ln * jax.nn.sigmoid(ln)).astype(BF16)
    y_c = _dot(act, wpw_ref[...])
    y_a = _dot(attn_ref[0], womla_ref[...])
    gate = gate_ref[0]
    z = gate[:, :d].astype(F32) * y_a + gate[:, d:].astype(F32) * y_c
    zo = _dot(z.astype(BF16), wout_ref[...])
    m = mod_ref[0]
    o_ref[0] = x_ref[0] + m[:, 2 * d:3 * d] * zo


def _merge_call(x, mod3, attn, y, gate, wts, tm):
    b, l, d = x.shape
    nt = l // tm
    hb = tm // HALO
    nhb = l // HALO
    weights = [wts[k] for k in ("womla", "wpw", "wout", "cw", "cb", "lng", "lnb")]
    tok = lambda w: pl.BlockSpec((1, tm, w), lambda bi, i: (bi, i, 0))
    return pl.pallas_call(
        functools.partial(_merge_kernel, rows=32),
        grid=(b, nt),
        in_specs=[tok(d),
                  pl.BlockSpec((1, 1, mod3.shape[-1]), lambda bi, i: (bi, 0, 0)),
                  tok(d), tok(d),
                  pl.BlockSpec((1, HALO, d), lambda bi, i: (bi, jnp.maximum(i * hb - 1, 0), 0)),
                  pl.BlockSpec((1, HALO, d), lambda bi, i: (bi, jnp.minimum((i + 1) * hb, nhb - 1), 0)),
                  tok(2 * d)] + [_const_spec(w) for w in weights],
        out_specs=tok(d),
        out_shape=jax.ShapeDtypeStruct((b, l, d), F32),
        scratch_shapes=[pltpu.VMEM((tm + 2 * HALO, d), F32),
                        pltpu.VMEM((SUBLANES, tm + 2 * HALO - SUBLANES, d), F32),
                        pltpu.VMEM((tm, d), F32)],
        compiler_params=_cparams(("parallel", "parallel")),
        name="merge",
    )(x, mod3, attn, y, y, y, gate, *weights)


def _topk_rank(s):
    nk = s.shape[0]
    key_id = lax.broadcasted_iota(jnp.int32, s.shape, 0)
    rank = jnp.full(s.shape, float(P_TOPK), F32)
    top_id = lax.broadcasted_iota(jnp.int32, (P_TOPK, s.shape[1]), 0)
    tops = jnp.zeros((P_TOPK, s.shape[1]), F32)
    for r in range(P_TOPK):
        cur = jnp.max(s, axis=0, keepdims=True)
        first = jnp.min(jnp.where(s == cur, key_id, nk), axis=0, keepdims=True)
        hit = key_id == first
        rank = jnp.where(hit, float(r), rank)
        s = jnp.where(hit, -jnp.inf, s)
        tops = jnp.where(top_id == r, cur, tops)
    return tops, rank


def _staircase(a, b):
    row_id = lax.broadcasted_iota(jnp.int32, a.shape, 0)
    cnt = jnp.zeros(a.shape, F32)
    front = a + b[0:1]
    for _ in range(P_TOPK):
        best = jnp.max(front, axis=0, keepdims=True)
        pick = jnp.min(jnp.where(front == best, row_id, P_TOPK), axis=0, keepdims=True)
        hit = row_id == pick
        cnt = jnp.where(hit, cnt + 1.0, cnt)
        nxt = jnp.full(a.shape, -jnp.inf, F32)
        for k in range(1, P_TOPK):
            nxt = jnp.where(cnt == float(k), b[k:k + 1], nxt)
        front = jnp.where(hit, a + nxt, front)
    return cnt


def _peer_kernel(x_ref, mod_ref, g2_ref, wpqt_ref, keys_ref, u_ref, vt_ref, fg_ref, o_ref,
                 h2_ref, sa_ref, rc_ref, top_ref, b2b_ref, rk2b_ref, at0_ref, at1_ref, gt0_ref, gt1_ref,
                 acc_ref):
    s = pl.program_id(1)
    n_steps = pl.num_programs(1)
    tt, d = x_ref.shape
    nlt = tt // LANES
    nk = keys_ref.shape[1]
    et = u_ref.shape[0] // 2
    ke = et // nk
    m = mod_ref[0]

    @pl.when(s == 0)
    def _prepare():
        h2 = _modulated_norm(x_ref[...], g2_ref[...], m[:, 3 * d:4 * d], m[:, 4 * d:5 * d]).astype(BF16)
        h2_ref[...] = h2
        dk = keys_ref.shape[2]
        qt = _dot_nt(wpqt_ref[...], h2).astype(BF16)
        for hp in range(2 * P_HEADS):
            sc = _dot(keys_ref[hp], qt[hp * dk:(hp + 1) * dk, :])
            for lt in range(nlt):
                sa_ref[hp, lt] = sc[:, lt * LANES:(lt + 1) * LANES]

        n_rank = 2 * P_HEADS * nlt

        def rank_units(u, _):
            for v in (u, u + n_rank // 2):
                hp, lt = v // nlt, v % nlt
                tops, rank = _topk_rank(sa_ref[hp, lt])
                top_ref[hp, lt] = tops
                rc_ref[hp, lt] = rank
            return 0

        lax.fori_loop(0, n_rank // 2, rank_units, 0)

        n_gate = P_HEADS * nlt

        def gate_units(u, _):
            for v in (u, u + n_gate // 2):
                h, lt = v // nlt, v % nlt
                a, b = top_ref[2 * h, lt], top_ref[2 * h + 1, lt]
                cnt = _staircase(a, b)
                ea = jnp.exp(a - a[0:1])
                eb = jnp.exp(b - b[0:1])
                cum = jnp.zeros(a.shape, F32)
                for k in range(P_TOPK):
                    cum = cum + jnp.where(cnt > float(k), eb[k:k + 1], 0.0)
                inv_z = 1.0 / jnp.sum(ea * cum, axis=0, keepdims=True)
                r1 = rc_ref[2 * h, lt]
                cnt1 = jnp.zeros(r1.shape, F32)
                for k in range(P_TOPK):
                    cnt1 = jnp.where(r1 == float(k), cnt[k:k + 1], cnt1)
                rc_ref[2 * h, lt] = cnt1
                sa_ref[2 * h, lt] = jnp.exp(sa_ref[2 * h, lt] - a[0:1]) * inv_z
                b2b_ref[h, lt] = jnp.exp(sa_ref[2 * h + 1, lt] - b[0:1]).astype(BF16)
                rk2b_ref[h, lt] = rc_ref[2 * h + 1, lt].astype(BF16)
            return 0

        lax.fori_loop(0, n_gate // 2, gate_units, 0)
        acc_ref[...] = jnp.zeros(acc_ref.shape, F32)
        for ref in (at0_ref, at1_ref, gt0_ref, gt1_ref):
            ref[...] = jnp.zeros(ref.shape, ref.dtype)

    def stage_a(half, dst):
        def piece(c):
            rows = slice(c * MM_ROWS, (c + 1) * MM_ROWS)
            dst[rows, :] = _dot_nt(u_ref[half * et + c * MM_ROWS:half * et + (c + 1) * MM_ROWS, :], h2_ref[...])
        return [functools.partial(piece, c) for c in range(et // MM_ROWS)]

    def stage_b(tile, src, dst):
        e1_base = jnp.clip(tile * ke, 0, nk - ke)
        groups = nk // BF16_ROWS

        def piece(lt):
            ls = slice(lt * LANES, (lt + 1) * LANES)
            ws = [[jnp.zeros((BF16_ROWS, LANES), BF16) for _ in range(groups)] for _ in range(ke)]
            for h in range(P_HEADS):
                b2 = b2b_ref[h, lt]
                rk2 = rk2b_ref[h, lt]
                for r in range(ke):
                    cnt_row = rc_ref[2 * h, lt, pl.ds(e1_base + r, 1), :]
                    a_row = sa_ref[2 * h, lt, pl.ds(e1_base + r, 1), :]
                    cnt_t = jnp.broadcast_to(cnt_row, (BF16_ROWS, LANES)).astype(BF16)
                    a_t = jnp.broadcast_to(a_row, (BF16_ROWS, LANES)).astype(BF16)
                    for g in range(groups):
                        gs = slice(g * BF16_ROWS, (g + 1) * BF16_ROWS)
                        ws[r][g] = ws[r][g] + jnp.where(rk2[gs] < cnt_t, a_t * b2[gs], jnp.zeros_like(a_t))
            for r in range(ke):
                er = slice(r * nk, (r + 1) * nk)
                pre = src[er, ls]
                act = 0.5 * pre * (1.0 + lax.erf(pre * np.float32(2.0 ** -0.5)))
                dst[er, ls] = act.astype(BF16) * jnp.concatenate(ws[r], axis=0)
        return [functools.partial(piece, lt) for lt in range(nlt)]

    def stage_c(half, src):
        def piece(c):
            rows = slice(c * MM_ROWS, (c + 1) * MM_ROWS)
            acc_ref[rows, :] += _dot(vt_ref[rows, half * et:(half + 1) * et], src[...])
        return [functools.partial(piece, c) for c in range(d // MM_ROWS)]

    def run_interleaved(vector_pieces, matmul_pieces):
        n_v, n_m = len(vector_pieces), len(matmul_pieces)
        done = 0
        for i, vp in enumerate(vector_pieces):
            want = ((i + 1) * n_m + n_v - 1) // n_v
            for mp in matmul_pieces[done:want]:
                mp()
            done = max(done, want)
            vp()

    run_interleaved(stage_b(2 * s - 1, at1_ref, gt1_ref), stage_c(0, gt0_ref) + stage_a(0, at0_ref))
    run_interleaved(stage_b(2 * s, at0_ref, gt0_ref), stage_c(1, gt1_ref) + stage_a(1, at1_ref))

    @pl.when(s == n_steps - 1)
    def _finish():
        x2 = x_ref[...] + m[:, 5 * d:6 * d] * acc_ref[...].T
        o_ref[...] = _rms(x2) * fg_ref[...]


def _peer_call(x1, mod3, g2, wpqt, keys, u_b, vt_b, fg, tokens_per_batch, tt, et):
    n, d = x1.shape
    ne = u_b.shape[0]
    nk = keys.shape[1]
    nlt = tt // LANES
    n_pairs = ne // (2 * et)
    tiles_per_batch = tokens_per_batch // tt
    unit = (2 * P_HEADS, nlt, nk, LANES)
    return pl.pallas_call(
        _peer_kernel,
        grid=(n // tt, n_pairs + 1),
        in_specs=[pl.BlockSpec((tt, d), lambda t, j: (t, 0)),
                  pl.BlockSpec((1, 1, mod3.shape[-1]), lambda t, j: (t // tiles_per_batch, 0, 0)),
                  _const_spec(g2), _const_spec(wpqt), _const_spec(keys),
                  pl.BlockSpec((2 * et, d), lambda t, j: (jnp.minimum(j, n_pairs - 1), 0)),
                  pl.BlockSpec((d, 2 * et), lambda t, j: (0, jnp.maximum(j - 1, 0))),
                  _const_spec(fg)],
        out_specs=pl.BlockSpec((tt, d), lambda t, j: (t, 0)),
        out_shape=jax.ShapeDtypeStruct((n, d), F32),
        scratch_shapes=[pltpu.VMEM((tt, d), BF16),
                        pltpu.VMEM(unit, F32),
                        pltpu.VMEM(unit, F32),
                        pltpu.VMEM((2 * P_HEADS, nlt, P_TOPK, LANES), F32),
                        pltpu.VMEM((P_HEADS, nlt, nk, LANES), BF16),
                        pltpu.VMEM((P_HEADS, nlt, nk, LANES), BF16),
                        pltpu.VMEM((et, tt), F32), pltpu.VMEM((et, tt), F32),
                        pltpu.VMEM((et, tt), BF16), pltpu.VMEM((et, tt), BF16),
                        pltpu.VMEM((d, tt), F32)],
        compiler_params=_cparams(("parallel", "arbitrary")),
        name="peer",
    )(x1, mod3, g2, wpqt, keys, u_b, vt_b, fg)


def _rope_tables(n_tok):
    rows = n_tok // GRID_W
    r = jnp.repeat(jnp.arange(rows, dtype=jnp.int32), GRID_W)
    col = jnp.tile(jnp.arange(GRID_W, dtype=jnp.int32), rows)
    n_freq = ROPE_DIM // 4
    freq = ROPE_THETA ** (-jnp.arange(n_freq, dtype=F32) / n_freq)
    ang = jnp.stack([r[:, None] * freq, col[:, None] * freq], axis=1)
    dd = np.arange(ROPE_DIM)
    axis, half, fr = dd // (ROPE_DIM // 2), (dd % (ROPE_DIM // 2)) // n_freq, dd % n_freq
    c32 = jnp.cos(ang)[:, axis, fr]
    s32 = jnp.sin(ang)[:, axis, fr] * jnp.where(half == 0, -1.0, 1.0).astype(F32)
    pad = jnp.zeros((n_tok, HEAD_PAD - NOPE_DIM - ROPE_DIM), F32)
    ones = jnp.ones((n_tok, NOPE_DIM), F32)
    ck = jnp.concatenate([ones, c32, pad], axis=1)
    sk = jnp.concatenate([0.0 * ones, s32, pad], axis=1)
    q_scale = ATTN_SCALE * LOG2_E
    return {"cq": ck * q_scale, "sq": sk * q_scale, "ck": ck, "sk": sk}


def _prep_weights(w_in, q_norm_g, w_uq, kv_norm_g, w_ukv, w_o_mla, conv_w, conv_b, conv_ln_g,
                  conv_ln_b, w_pw, w_out):
    d = w_in.shape[0]
    q_rank = q_norm_g.shape[0]
    kv_rank = kv_norm_g.shape[0]
    off_kv = q_rank
    off_kr = q_rank + kv_rank
    off_conv = off_kr + ROPE_DIM
    off_gate = off_conv + 2 * d
    qk = NOPE_DIM + ROPE_DIM
    tail = HEAD_PAD - qk

    def lane_pad(w, before, after):
        return jnp.pad(w, [(0, 0)] * (w.ndim - 1) + [(before, after)])

    def partner(w):
        g = w.reshape(w.shape[:-1] + (2, 2, ROPE_DIM // 4))
        return g[..., ::-1, :].reshape(w.shape)

    def place_rope(w32):
        return lane_pad(w32, NOPE_DIM, tail)

    wkr32 = w_in[:, off_kr:off_conv]
    uq = w_uq.reshape(q_rank, N_HEADS, qk)
    uq_pad = lane_pad(uq, 0, tail)
    uq_sw = lane_pad(partner(uq[:, :, NOPE_DIM:]), NOPE_DIM, tail)
    ukv = w_ukv.reshape(kv_rank, N_HEADS, NOPE_DIM + V_DIM)
    uk_pad = lane_pad(ukv[:, :, :NOPE_DIM], 0, HEAD_PAD - NOPE_DIM)
    uv = ukv[:, :, NOPE_DIM:]
    even = (np.arange(N_HEADS) % 2 == 0)[None, :, None]
    uv_pad = jnp.concatenate([jnp.where(even, uv, 0.0), jnp.where(even, 0.0, uv)], axis=-1)
    hp = N_HEADS * HEAD_PAD
    row = lambda v: v.reshape(1, -1).astype(F32)
    return {
        "wq": w_in[:, :off_kv].astype(BF16),
        "wkvc": w_in[:, off_kv:off_kr].astype(BF16),
        "wkr": place_rope(wkr32).astype(BF16),
        "wkrs": place_rope(partner(wkr32)).astype(BF16),
        "wca": w_in[:, off_conv:off_conv + d].astype(BF16),
        "wcg": w_in[:, off_conv + d:off_gate].astype(BF16),
        "wgate": w_in[:, off_gate:].astype(BF16),
        "qg": row(q_norm_g),
        "wuq": uq_pad.reshape(q_rank, hp).astype(BF16),
        "wuqs": uq_sw.reshape(q_rank, hp).astype(BF16),
        "kvg": row(kv_norm_g),
        "wukv": jnp.concatenate([uk_pad.reshape(kv_rank, hp), uv_pad.reshape(kv_rank, hp)], axis=1).astype(BF16),
        "womla": w_o_mla.astype(BF16),
        "wpw": w_pw.astype(BF16),
        "wout": w_out.astype(BF16),
        "cw": conv_w.astype(F32),
        "cb": row(conv_b),
        "lng": row(conv_ln_g),
        "lnb": row(conv_ln_b),
    }


def kernel(x, c, ctx, c_ctx, w_mod, b_mod, norm1_g, norm2_g, w_in, q_norm_g, w_uq, kv_norm_g, w_ukv,
           w_o_mla, conv_w, conv_b, conv_ln_g, conv_ln_b, w_pw, w_out, w_pq, sub_keys, u_experts,
           v_experts, final_g):
    assert w_mod.shape[0] == 1, "single-layer configuration"
    b, l, d = x.shape
    assert c.shape[0] + 1 <= 8
    row = lambda v: v.reshape(1, -1).astype(F32)

    cvec = jnp.concatenate([c, c_ctx[None, :], jnp.zeros((SUBLANES - b - 1, d), F32)], axis=0)
    mod = _mod_call(cvec, w_mod[0], b_mod[0].reshape(1, -1))
    mod3 = mod.reshape(8, 1, 6 * d)

    wts = _prep_weights(w_in[0], q_norm_g[0], w_uq[0], kv_norm_g[0], w_ukv[0], w_o_mla[0], conv_w[0],
                        conv_b[0], conv_ln_g[0], conv_ln_b[0], w_pw[0], w_out[0])
    tabs = _rope_tables(l)
    g1 = row(norm1_g[0])

    tm = min(256, l)
    q, kl, vl, y, gate = _inproj_call(x, mod3, g1, wts, tabs, tm)
    kc, vc = _ctxkv_call(ctx, mod3, b, g1, wts)
    attn = _attn_call(q, kl, vl, kc, vc, tq=min(512, l), tk=min(512, l))
    x1 = _merge_call(x, mod3, attn, y, gate, wts, tm=min(256, l))

    n_keys = sub_keys.shape[3]
    keys = sub_keys[0].reshape(2 * P_HEADS, n_keys, -1).astype(BF16)
    wpqt = w_pq[0].T.astype(BF16)
    u_b = u_experts[0].astype(BF16)
    vt_b = v_experts[0].T.astype(BF16)
    out = _peer_call(x1.reshape(b * l, d), mod3, row(norm2_g[0]), wpqt, keys, u_b, vt_b, row(final_g),
                     tokens_per_batch=l, tt=min(512, l), et=512)
    return out.reshape(b, l, d)
```

```python
import functools

import jax
import jax.numpy as jnp
import numpy as np
from jax import lax
from jax.experimental import pallas as pl
from jax.experimental.pallas import tpu as pltpu

F32 = jnp.float32
BF16 = jnp.bfloat16

EPS = 1e-6
GRID_W = 64
N_HEADS = 16
NOPE_DIM = 64
ROPE_DIM = 32
V_DIM = 64
ROPE_THETA = 10000.0
ATTN_SCALE = (NOPE_DIM + ROPE_DIM) ** -0.5
LOG2_E = 1.4426950408889634
CONV_WIDTH = 31
CONV_HALF = CONV_WIDTH // 2
P_HEADS = 8
P_TOPK = 16

LANES = 128
SUBLANES = 8
BF16_ROWS = 16
HEAD_PAD = LANES
HALO = 16
R_PER_PIECE = 2
VMEM_LIMIT = 56 * 1024 * 1024

_NT = (((1,), (1,)), ((), ()))


def _dot(a, b):
    return jnp.dot(a, b, preferred_element_type=F32)


def _dot_nt(a, b):
    return lax.dot_general(a, b, _NT, preferred_element_type=F32)


def _rms(x):
    return x * lax.rsqrt(jnp.mean(x * x, axis=-1, keepdims=True) + EPS)


def _cparams(sem):
    return pltpu.CompilerParams(dimension_semantics=sem, vmem_limit_bytes=VMEM_LIMIT)


def _mod_kernel(c_ref, w_ref, b_ref, o_ref):
    cv = c_ref[...]
    s = cv * jax.nn.sigmoid(cv)
    o_ref[...] = jnp.dot(s, w_ref[...], preferred_element_type=F32,
                         precision=lax.Precision.HIGHEST) + b_ref[...]


def _mod_call(cvec, w_mod, b_mod):
    rows, d = cvec.shape
    n = w_mod.shape[1]
    tn = d
    return pl.pallas_call(
        _mod_kernel,
        grid=(n // tn,),
        in_specs=[pl.BlockSpec((rows, d), lambda j: (0, 0)),
                  pl.BlockSpec((d, tn), lambda j: (0, j)),
                  pl.BlockSpec((1, tn), lambda j: (0, j))],
        out_specs=pl.BlockSpec((rows, tn), lambda j: (0, j)),
        out_shape=jax.ShapeDtypeStruct((rows, n), F32),
        compiler_params=_cparams(("arbitrary",)),
        name="mod",
    )(cvec, w_mod, b_mod)


def _modulated_norm(x, g, shift, scale):
    return (_rms(x) * g) * (1.0 + scale) + shift


def _kv_from_h(hb, wkvc_ref, kvg_ref, wukv_ref):
    ckv = _dot(hb, wkvc_ref[...])
    ckvn = (_rms(ckv) * kvg_ref[...]).astype(BF16)
    return _dot(ckvn, wukv_ref[...])


def _inproj_kernel(x_ref, mod_ref, g1_ref, wq_ref, wkvc_ref, wkr_ref, wkrs_ref, wca_ref, wcg_ref,
                   wgate_ref, qg_ref, wuq_ref, wuqs_ref, kvg_ref, wukv_ref,
                   cq_ref, sq_ref, ck_ref, sk_ref,
                   q_out, k_out, v_out, y_out, gate_out):
    d = x_ref.shape[-1]
    hp = N_HEADS * HEAD_PAD
    m = mod_ref[0]
    hb = _modulated_norm(x_ref[0], g1_ref[...], m[:, 0:d], m[:, d:2 * d]).astype(BF16)

    cqn = (_rms(_dot(hb, wq_ref[...])) * qg_ref[...]).astype(BF16)
    qa = _dot(cqn, wuq_ref[...])
    qb = _dot(cqn, wuqs_ref[...])
    cq, sq = cq_ref[...], sq_ref[...]
    for h in range(N_HEADS):
        sl = slice(h * HEAD_PAD, (h + 1) * HEAD_PAD)
        q_out[0, :, sl] = (qa[:, sl] * cq + qb[:, sl] * sq).astype(BF16)

    kv = _kv_from_h(hb, wkvc_ref, kvg_ref, wukv_ref)
    kr = _dot(hb, wkr_ref[...]) * ck_ref[...] + _dot(hb, wkrs_ref[...]) * sk_ref[...]
    for h in range(N_HEADS):
        sl = slice(h * HEAD_PAD, (h + 1) * HEAD_PAD)
        k_out[0, :, sl] = (kv[:, sl] + kr).astype(BF16)
    v_out[0] = kv[:, hp:].astype(BF16)

    a = _dot(hb, wca_ref[...])
    g = _dot(hb, wcg_ref[...])
    y_out[0] = (a * jax.nn.sigmoid(g)).astype(BF16)
    gate_out[0] = jax.nn.sigmoid(_dot(hb, wgate_ref[...])).astype(BF16)


def _const_spec(arr):
    nd = arr.ndim
    return pl.BlockSpec(arr.shape, lambda *_: (0,) * nd)


def _inproj_call(x, mod3, g1, wts, tabs, tm):
    b, l, d = x.shape
    hp = N_HEADS * HEAD_PAD
    nt = l // tm
    weights = [wts[k] for k in ("wq", "wkvc", "wkr", "wkrs", "wca", "wcg", "wgate",
                                "qg", "wuq", "wuqs", "kvg", "wukv")]
    tok = lambda w: pl.BlockSpec((1, tm, w), lambda bi, i: (bi, i, 0))
    tab = pl.BlockSpec((tm, HEAD_PAD), lambda bi, i: (i, 0))
    return pl.pallas_call(
        _inproj_kernel,
        grid=(b, nt),
        in_specs=[tok(d),
                  pl.BlockSpec((1, 1, mod3.shape[-1]), lambda bi, i: (bi, 0, 0)),
                  _const_spec(g1)] + [_const_spec(w) for w in weights] + [tab] * 4,
        out_specs=[tok(hp), tok(hp), tok(hp), tok(d), tok(2 * d)],
        out_shape=[jax.ShapeDtypeStruct((b, l, hp), BF16),
                   jax.ShapeDtypeStruct((b, l, hp), BF16),
                   jax.ShapeDtypeStruct((b, l, hp), BF16),
                   jax.ShapeDtypeStruct((b, l, d), BF16),
                   jax.ShapeDtypeStruct((b, l, 2 * d), BF16)],
        compiler_params=_cparams(("parallel", "parallel")),
        name="in_proj",
    )(x, mod3, g1, *weights, tabs["cq"], tabs["sq"], tabs["ck"], tabs["sk"])


def _ctxkv_kernel(x_ref, mod_ref, g1_ref, wkvc_ref, wkr_ref, kvg_ref, wukv_ref, k_out, v_out):
    d = x_ref.shape[-1]
    hp = N_HEADS * HEAD_PAD
    m = mod_ref[0]
    hb = _modulated_norm(x_ref[0], g1_ref[...], m[:, 0:d], m[:, d:2 * d]).astype(BF16)
    kv = _kv_from_h(hb, wkvc_ref, kvg_ref, wukv_ref)
    kr = _dot(hb, wkr_ref[...])
    for h in range(N_HEADS):
        sl = slice(h * HEAD_PAD, (h + 1) * HEAD_PAD)
        k_out[0, :, sl] = (kv[:, sl] + kr).astype(BF16)
    v_out[0] = kv[:, hp:].astype(BF16)


def _ctxkv_call(ctx, mod3, ctx_row, g1, wts):
    b, n, d = ctx.shape
    hp = N_HEADS * HEAD_PAD
    weights = [wts[k] for k in ("wkvc", "wkr", "kvg", "wukv")]
    tok = lambda w: pl.BlockSpec((1, n, w), lambda bi: (bi, 0, 0))
    return pl.pallas_call(
        _ctxkv_kernel,
        grid=(b,),
        in_specs=[tok(d),
                  pl.BlockSpec((1, 1, mod3.shape[-1]), lambda bi: (ctx_row, 0, 0)),
                  _const_spec(g1)] + [_const_spec(w) for w in weights],
        out_specs=[tok(hp), tok(hp)],
        out_shape=[jax.ShapeDtypeStruct((b, n, hp), BF16)] * 2,
        compiler_params=_cparams(("parallel",)),
        name="ctx_kv",
    )(ctx, mod3, g1, *weights)


def _attn_kernel(q_ref, kl_ref, vl_ref, kc_ref, vc_ref, o_ref, m_ref, l_ref, acc_ref, *, tk):
    n_chunks = kl_ref.shape[1] // tk
    m_ref[...] = jnp.full(m_ref.shape, -jnp.inf, F32)
    l_ref[...] = jnp.zeros(l_ref.shape, F32)
    acc_ref[...] = jnp.zeros(acc_ref.shape, F32)

    def update(head, k, v):
        sl = slice(head * HEAD_PAD, (head + 1) * HEAD_PAD)
        s = _dot_nt(q_ref[0, :, sl], k)
        tiles = [s[:, t * LANES:(t + 1) * LANES] for t in range(s.shape[1] // LANES)]
        tile_max = functools.reduce(jnp.maximum, tiles)
        m_old = m_ref[head]
        m_new = jnp.maximum(m_old, jnp.max(tile_max, axis=-1, keepdims=True))
        alpha = jnp.exp2(m_old - m_new)
        ps = [jnp.exp2(t - m_new) for t in tiles]
        l_ref[head] = alpha * l_ref[head] + functools.reduce(jnp.add, ps)
        p = jnp.concatenate([t.astype(BF16) for t in ps], axis=1)
        acc_ref[head] = alpha * acc_ref[head] + _dot(p, v)
        m_ref[head] = m_new

    def body(c, _):
        off = pl.multiple_of(c * tk, tk)
        for head in range(2):
            sl = slice(head * HEAD_PAD, (head + 1) * HEAD_PAD)
            update(head, kl_ref[0, pl.ds(off, tk), sl], vl_ref[0, pl.ds(off, tk), sl])
        return 0

    lax.fori_loop(0, n_chunks, body, 0)
    out = None
    for head in range(2):
        sl = slice(head * HEAD_PAD, (head + 1) * HEAD_PAD)
        update(head, kc_ref[0, :, sl], vc_ref[0, :, sl])
        o = acc_ref[head] * (1.0 / jnp.sum(l_ref[head], axis=-1, keepdims=True))
        out = o if out is None else out + o
    o_ref[0] = out.astype(BF16)


def _attn_call(q, kl, vl, kc, vc, tq, tk):
    b, l, hp = q.shape
    nc = kc.shape[1]
    pairs = N_HEADS // 2
    pw = 2 * HEAD_PAD
    return pl.pallas_call(
        functools.partial(_attn_kernel, tk=tk),
        grid=(b, pairs, l // tq),
        in_specs=[pl.BlockSpec((1, tq, pw), lambda bi, p, i: (bi, i, p)),
                  pl.BlockSpec((1, l, pw), lambda bi, p, i: (bi, 0, p)),
                  pl.BlockSpec((1, l, pw), lambda bi, p, i: (bi, 0, p)),
                  pl.BlockSpec((1, nc, pw), lambda bi, p, i: (bi, 0, p)),
                  pl.BlockSpec((1, nc, pw), lambda bi, p, i: (bi, 0, p))],
        out_specs=pl.BlockSpec((1, tq, HEAD_PAD), lambda bi, p, i: (bi, i, p)),
        out_shape=jax.ShapeDtypeStruct((b, l, pairs * HEAD_PAD), BF16),
        scratch_shapes=[pltpu.VMEM((2, tq, HEAD_PAD), F32)] * 3,
        compiler_params=_cparams(("parallel", "parallel", "parallel")),
        name="attention",
    )(q, kl, vl, kc, vc)


def _merge_kernel(x_ref, mod_ref, attn_ref, y_ref, yp_ref, yn_ref, gate_ref,
                  womla_ref, wpw_ref, wout_ref, cw_ref, cb_ref, lng_ref, lnb_ref,
                  o_ref, ybuf, sbuf, cbuf, *, rows):
    i = pl.program_id(1)
    last = pl.num_programs(1) - 1
    tm, d = y_ref.shape[1], y_ref.shape[2]

    prev = yp_ref[0].astype(F32)
    nxt = yn_ref[0].astype(F32)
    ybuf[0:HALO, :] = jnp.where(i > 0, prev, 0.0)
    ybuf[HALO:HALO + tm, :] = y_ref[0].astype(F32)
    ybuf[HALO + tm:2 * HALO + tm, :] = jnp.where(i < last, nxt, 0.0)
    span = sbuf.shape[1]
    for p in range(SUBLANES):
        sbuf[p] = ybuf[p:p + span, :]

    cb = cb_ref[...]

    def conv_rows(r, _):
        base = pl.multiple_of(r * rows, rows)
        acc = jnp.broadcast_to(cb, (rows, d))
        for k in range(CONV_WIDTH):
            q, p = divmod(HALO - CONV_HALF + k, SUBLANES)
            acc = acc + cw_ref[k:k + 1, :] * sbuf[p, pl.ds(base + q * SUBLANES, rows), :]
        cbuf[pl.ds(base, rows), :] = acc
        return 0

    lax.fori_loop(0, tm // rows, conv_rows, 0)

    yc = cbuf[...]
    mu = jnp.mean(yc, axis=-1, keepdims=True)
    cen = yc - mu
    var = jnp.mean(cen * cen, axis=-1, keepdims=True)
    ln = cen * lax.rsqrt(var + EPS) * lng_ref[...] + lnb_ref[...]
    act = (ln * jax.nn.sigmoid(ln)).astype(BF16)
    y_c = _dot(act, wpw_ref[...])
    y_a = _dot(attn_ref[0], womla_ref[...])
    gate = gate_ref[0]
    z = gate[:, :d].astype(F32) * y_a + gate[:, d:].astype(F32) * y_c
    zo = _dot(z.astype(BF16), wout_ref[...])
    m = mod_ref[0]
    o_ref[0] = x_ref[0] + m[:, 2 * d:3 * d] * zo


def _merge_call(x, mod3, attn, y, gate, wts, tm):
    b, l, d = x.shape
    nt = l // tm
    hb = tm // HALO
    nhb = l // HALO
    weights = [wts[k] for k in ("womla", "wpw", "wout", "cw", "cb", "lng", "lnb")]
    tok = lambda w: pl.BlockSpec((1, tm, w), lambda bi, i: (bi, i, 0))
    return pl.pallas_call(
        functools.partial(_merge_kernel, rows=32),
        grid=(b, nt),
        in_specs=[tok(d),
                  pl.BlockSpec((1, 1, mod3.shape[-1]), lambda bi, i: (bi, 0, 0)),
                  tok(d), tok(d),
                  pl.BlockSpec((1, HALO, d), lambda bi, i: (bi, jnp.maximum(i * hb - 1, 0), 0)),
                  pl.BlockSpec((1, HALO, d), lambda bi, i: (bi, jnp.minimum((i + 1) * hb, nhb - 1), 0)),
                  tok(2 * d)] + [_const_spec(w) for w in weights],
        out_specs=tok(d),
        out_shape=jax.ShapeDtypeStruct((b, l, d), F32),
        scratch_shapes=[pltpu.VMEM((tm + 2 * HALO, d), F32),
                        pltpu.VMEM((SUBLANES, tm + 2 * HALO - SUBLANES, d), F32),
                        pltpu.VMEM((tm, d), F32)],
        compiler_params=_cparams(("parallel", "parallel")),
        name="merge",
    )(x, mod3, attn, y, y, y, gate, *weights)


def _topk_rank(s):
    nk = s.shape[0]
    key_id = lax.broadcasted_iota(jnp.int32, s.shape, 0)
    rank = jnp.full(s.shape, float(P_TOPK), F32)
    top_id = lax.broadcasted_iota(jnp.int32, (P_TOPK, s.shape[1]), 0)
    tops = jnp.zeros((P_TOPK, s.shape[1]), F32)
    for r in range(P_TOPK):
        cur = jnp.max(s, axis=0, keepdims=True)
        first = jnp.min(jnp.where(s == cur, key_id, nk), axis=0, keepdims=True)
        hit = key_id == first
        rank = jnp.where(hit, float(r), rank)
        s = jnp.where(hit, -jnp.inf, s)
        tops = jnp.where(top_id == r, cur, tops)
    return tops, rank


def _staircase(a, b):
    row_id = lax.broadcasted_iota(jnp.int32, a.shape, 0)
    cnt = jnp.zeros(a.shape, F32)
    front = a + b[0:1]
    for _ in range(P_TOPK):
        best = jnp.max(front, axis=0, keepdims=True)
        pick = jnp.min(jnp.where(front == best, row_id, P_TOPK), axis=0, keepdims=True)
        hit = row_id == pick
        cnt = jnp.where(hit, cnt + 1.0, cnt)
        nxt = jnp.full(a.shape, -jnp.inf, F32)
        for k in range(1, P_TOPK):
            nxt = jnp.where(cnt == float(k), b[k:k + 1], nxt)
        front = jnp.where(hit, a + nxt, front)
    return cnt


def _peer_kernel(x_ref, mod_ref, g2_ref, wpqt_ref, keys_ref, u_ref, vt_ref, fg_ref, o_ref,
                 h2_ref, sa_ref, rc_ref, top_ref, b2b_ref, rk2b_ref, at0_ref, at1_ref, gt0_ref, gt1_ref,
                 acc_ref):
    s = pl.program_id(1)
    n_steps = pl.num_programs(1)
    tt, d = x_ref.shape
    nlt = tt // LANES
    nk = keys_ref.shape[1]
    et = u_ref.shape[0] // 2
    ke = et // nk
    m = mod_ref[0]

    @pl.when(s == 0)
    def _prepare():
        h2 = _modulated_norm(x_ref[...], g2_ref[...], m[:, 3 * d:4 * d], m[:, 4 * d:5 * d]).astype(BF16)
        h2_ref[...] = h2
        dk = keys_ref.shape[2]
        qt = _dot_nt(wpqt_ref[...], h2).astype(BF16)
        for hp in range(2 * P_HEADS):
            sc = _dot(keys_ref[hp], qt[hp * dk:(hp + 1) * dk, :])
            for lt in range(nlt):
                sa_ref[hp, lt] = sc[:, lt * LANES:(lt + 1) * LANES]

        n_rank = 2 * P_HEADS * nlt

        def rank_units(u, _):
            for v in (u, u + n_rank // 2):
                hp, lt = v // nlt, v % nlt
                tops, rank = _topk_rank(sa_ref[hp, lt])
                top_ref[hp, lt] = tops
                rc_ref[hp, lt] = rank
            return 0

        lax.fori_loop(0, n_rank // 2, rank_units, 0)

        n_gate = P_HEADS * nlt

        def gate_units(u, _):
            for v in (u, u + n_gate // 2):
                h, lt = v // nlt, v % nlt
                a, b = top_ref[2 * h, lt], top_ref[2 * h + 1, lt]
                cnt = _staircase(a, b)
                ea = jnp.exp(a - a[0:1])
                eb = jnp.exp(b - b[0:1])
                cum = jnp.zeros(a.shape, F32)
                for k in range(P_TOPK):
                    cum = cum + jnp.where(cnt > float(k), eb[k:k + 1], 0.0)
                inv_z = 1.0 / jnp.sum(ea * cum, axis=0, keepdims=True)
                r1 = rc_ref[2 * h, lt]
                cnt1 = jnp.zeros(r1.shape, F32)
                for k in range(P_TOPK):
                    cnt1 = jnp.where(r1 == float(k), cnt[k:k + 1], cnt1)
                rc_ref[2 * h, lt] = cnt1
                sa_ref[2 * h, lt] = jnp.exp(sa_ref[2 * h, lt] - a[0:1]) * inv_z
                b2b_ref[h, lt] = jnp.exp(sa_ref[2 * h + 1, lt] - b[0:1]).astype(BF16)
                rk2b_ref[h, lt] = rc_ref[2 * h + 1, lt].astype(BF16)
            return 0

        lax.fori_loop(0, n_gate // 2, gate_units, 0)
        acc_ref[...] = jnp.zeros(acc_ref.shape, F32)
        for ref in (at0_ref, at1_ref, gt0_ref, gt1_ref):
            ref[...] = jnp.zeros(ref.shape, ref.dtype)

    def half_step(half, tile_b, at_src, gt_dst, at_dst, gt_src):
        e1_base = jnp.clip(tile_b * ke, 0, nk - ke)
        groups = nk // BF16_ROWS
        n_trips = ke // R_PER_PIECE
        a_rows = et // n_trips
        c_rows = d // n_trips

        def trip(i, _):
            cr = pl.ds(pl.multiple_of(i * c_rows, c_rows), c_rows)
            acc_ref[cr, :] += _dot(vt_ref[cr, half * et:(half + 1) * et], gt_src[...])
            a_lo = pl.multiple_of(i * a_rows, a_rows)
            u_rows = pl.ds(pl.multiple_of(half * et + a_lo, a_rows), a_rows)
            at_dst[pl.ds(a_lo, a_rows), :] = _dot_nt(u_ref[u_rows, :], h2_ref[...])

            for lt in range(nlt):
                ls = slice(lt * LANES, (lt + 1) * LANES)
                ws = [[jnp.zeros((BF16_ROWS, LANES), BF16) for _ in range(groups)] for _ in range(R_PER_PIECE)]
                for h in range(P_HEADS):
                    b2 = b2b_ref[h, lt]
                    rk2 = rk2b_ref[h, lt]
                    for rr in range(R_PER_PIECE):
                        e1 = e1_base + i * R_PER_PIECE + rr
                        cnt_row = rc_ref[2 * h, lt, pl.ds(e1, 1), :]
                        a_row = sa_ref[2 * h, lt, pl.ds(e1, 1), :]
                        cnt_t = jnp.broadcast_to(cnt_row, (BF16_ROWS, LANES)).astype(BF16)
                        a_t = jnp.broadcast_to(a_row, (BF16_ROWS, LANES)).astype(BF16)
                        for g in range(groups):
                            gs = slice(g * BF16_ROWS, (g + 1) * BF16_ROWS)
                            ws[rr][g] = ws[rr][g] + jnp.where(rk2[gs] < cnt_t, a_t * b2[gs], jnp.zeros_like(a_t))
                for rr in range(R_PER_PIECE):
                    er = pl.ds(pl.multiple_of((i * R_PER_PIECE + rr) * nk, nk), nk)
                    pre = at_src[er, ls]
                    act = 0.5 * pre * (1.0 + lax.erf(pre * np.float32(2.0 ** -0.5)))
                    gt_dst[er, ls] = act.astype(BF16) * jnp.concatenate(ws[rr], axis=0)
            return 0

        lax.fori_loop(0, n_trips, trip, 0)

    half_step(0, 2 * s - 1, at1_ref, gt1_ref, at0_ref, gt0_ref)
    half_step(1, 2 * s, at0_ref, gt0_ref, at1_ref, gt1_ref)

    @pl.when(s == n_steps - 1)
    def _finish():
        x2 = x_ref[...] + m[:, 5 * d:6 * d] * acc_ref[...].T
        o_ref[...] = _rms(x2) * fg_ref[...]


def _peer_call(x1, mod3, g2, wpqt, keys, u_b, vt_b, fg, tokens_per_batch, tt, et):
    n, d = x1.shape
    ne = u_b.shape[0]
    nk = keys.shape[1]
    nlt = tt // LANES
    n_pairs = ne // (2 * et)
    tiles_per_batch = tokens_per_batch // tt
    unit = (2 * P_HEADS, nlt, nk, LANES)
    return pl.pallas_call(
        _peer_kernel,
        grid=(n // tt, n_pairs + 1),
        in_specs=[pl.BlockSpec((tt, d), lambda t, j: (t, 0)),
                  pl.BlockSpec((1, 1, mod3.shape[-1]), lambda t, j: (t // tiles_per_batch, 0, 0)),
                  _const_spec(g2), _const_spec(wpqt), _const_spec(keys),
                  pl.BlockSpec((2 * et, d), lambda t, j: (jnp.minimum(j, n_pairs - 1), 0)),
                  pl.BlockSpec((d, 2 * et), lambda t, j: (0, jnp.maximum(j - 1, 0))),
                  _const_spec(fg)],
        out_specs=pl.BlockSpec((tt, d), lambda t, j: (t, 0)),
        out_shape=jax.ShapeDtypeStruct((n, d), F32),
        scratch_shapes=[pltpu.VMEM((tt, d), BF16),
                        pltpu.VMEM(unit, F32),
                        pltpu.VMEM(unit, F32),
                        pltpu.VMEM((2 * P_HEADS, nlt, P_TOPK, LANES), F32),
                        pltpu.VMEM((P_HEADS, nlt, nk, LANES), BF16),
                        pltpu.VMEM((P_HEADS, nlt, nk, LANES), BF16),
                        pltpu.VMEM((et, tt), F32), pltpu.VMEM((et, tt), F32),
                        pltpu.VMEM((et, tt), BF16), pltpu.VMEM((et, tt), BF16),
                        pltpu.VMEM((d, tt), F32)],
        compiler_params=_cparams(("parallel", "arbitrary")),
        name="peer",
    )(x1, mod3, g2, wpqt, keys, u_b, vt_b, fg)


def _rope_tables(n_tok):
    rows = n_tok // GRID_W
    r = jnp.repeat(jnp.arange(rows, dtype=jnp.int32), GRID_W)
    col = jnp.tile(jnp.arange(GRID_W, dtype=jnp.int32), rows)
    n_freq = ROPE_DIM // 4
    freq = ROPE_THETA ** (-jnp.arange(n_freq, dtype=F32) / n_freq)
    ang = jnp.stack([r[:, None] * freq, col[:, None] * freq], axis=1)
    dd = np.arange(ROPE_DIM)
    axis, half, fr = dd // (ROPE_DIM // 2), (dd % (ROPE_DIM // 2)) // n_freq, dd % n_freq
    c32 = jnp.cos(ang)[:, axis, fr]
    s32 = jnp.sin(ang)[:, axis, fr] * jnp.where(half == 0, -1.0, 1.0).astype(F32)
    pad = jnp.zeros((n_tok, HEAD_PAD - NOPE_DIM - ROPE_DIM), F32)
    ones = jnp.ones((n_tok, NOPE_DIM), F32)
    ck = jnp.concatenate([ones, c32, pad], axis=1)
    sk = jnp.concatenate([0.0 * ones, s32, pad], axis=1)
    q_scale = ATTN_SCALE * LOG2_E
    return {"cq": ck * q_scale, "sq": sk * q_scale, "ck": ck, "sk": sk}


def _prep_weights(w_in, q_norm_g, w_uq, kv_norm_g, w_ukv, w_o_mla, conv_w, conv_b, conv_ln_g,
                  conv_ln_b, w_pw, w_out):
    d = w_in.shape[0]
    q_rank = q_norm_g.shape[0]
    kv_rank = kv_norm_g.shape[0]
    off_kv = q_rank
    off_kr = q_rank + kv_rank
    off_conv = off_kr + ROPE_DIM
    off_gate = off_conv + 2 * d
    qk = NOPE_DIM + ROPE_DIM
    tail = HEAD_PAD - qk

    def lane_pad(w, before, after):
        return jnp.pad(w, [(0, 0)] * (w.ndim - 1) + [(before, after)])

    def partner(w):
        g = w.reshape(w.shape[:-1] + (2, 2, ROPE_DIM // 4))
        return g[..., ::-1, :].reshape(w.shape)

    def place_rope(w32):
        return lane_pad(w32, NOPE_DIM, tail)

    wkr32 = w_in[:, off_kr:off_conv]
    uq = w_uq.reshape(q_rank, N_HEADS, qk)
    uq_pad = lane_pad(uq, 0, tail)
    uq_sw = lane_pad(partner(uq[:, :, NOPE_DIM:]), NOPE_DIM, tail)
    ukv = w_ukv.reshape(kv_rank, N_HEADS, NOPE_DIM + V_DIM)
    uk_pad = lane_pad(ukv[:, :, :NOPE_DIM], 0, HEAD_PAD - NOPE_DIM)
    uv = ukv[:, :, NOPE_DIM:]
    even = (np.arange(N_HEADS) % 2 == 0)[None, :, None]
    uv_pad = jnp.concatenate([jnp.where(even, uv, 0.0), jnp.where(even, 0.0, uv)], axis=-1)
    hp = N_HEADS * HEAD_PAD
    row = lambda v: v.reshape(1, -1).astype(F32)
    return {
        "wq": w_in[:, :off_kv].astype(BF16),
        "wkvc": w_in[:, off_kv:off_kr].astype(BF16),
        "wkr": place_rope(wkr32).astype(BF16),
        "wkrs": place_rope(partner(wkr32)).astype(BF16),
        "wca": w_in[:, off_conv:off_conv + d].astype(BF16),
        "wcg": w_in[:, off_conv + d:off_gate].astype(BF16),
        "wgate": w_in[:, off_gate:].astype(BF16),
        "qg": row(q_norm_g),
        "wuq": uq_pad.reshape(q_rank, hp).astype(BF16),
        "wuqs": uq_sw.reshape(q_rank, hp).astype(BF16),
        "kvg": row(kv_norm_g),
        "wukv": jnp.concatenate([uk_pad.reshape(kv_rank, hp), uv_pad.reshape(kv_rank, hp)], axis=1).astype(BF16),
        "womla": w_o_mla.astype(BF16),
        "wpw": w_pw.astype(BF16),
        "wout": w_out.astype(BF16),
        "cw": conv_w.astype(F32),
        "cb": row(conv_b),
        "lng": row(conv_ln_g),
        "lnb": row(conv_ln_b),
    }


def kernel(x, c, ctx, c_ctx, w_mod, b_mod, norm1_g, norm2_g, w_in, q_norm_g, w_uq, kv_norm_g, w_ukv,
           w_o_mla, conv_w, conv_b, conv_ln_g, conv_ln_b, w_pw, w_out, w_pq, sub_keys, u_experts,
           v_experts, final_g):
    assert w_mod.shape[0] == 1, "single-layer configuration"
    b, l, d = x.shape
    assert c.shape[0] + 1 <= 8
    row = lambda v: v.reshape(1, -1).astype(F32)

    cvec = jnp.concatenate([c, c_ctx[None, :], jnp.zeros((SUBLANES - b - 1, d), F32)], axis=0)
    mod = _mod_call(cvec, w_mod[0], b_mod[0].reshape(1, -1))
    mod3 = mod.reshape(8, 1, 6 * d)

    wts = _prep_weights(w_in[0], q_norm_g[0], w_uq[0], kv_norm_g[0], w_ukv[0], w_o_mla[0], conv_w[0],
                        conv_b[0], conv_ln_g[0], conv_ln_b[0], w_pw[0], w_out[0])
    tabs = _rope_tables(l)
    g1 = row(norm1_g[0])

    tm = min(256, l)
    q, kl, vl, y, gate = _inproj_call(x, mod3, g1, wts, tabs, tm)
    kc, vc = _ctxkv_call(ctx, mod3, b, g1, wts)
    attn = _attn_call(q, kl, vl, kc, vc, tq=min(512, l), tk=min(2048, l))
    x1 = _merge_call(x, mod3, attn, y, gate, wts, tm=min(256, l))

    n_keys = sub_keys.shape[3]
    keys = sub_keys[0].reshape(2 * P_HEADS, n_keys, -1).astype(BF16)
    wpqt = w_pq[0].T.astype(BF16)
    u_b = u_experts[0].astype(BF16)
    vt_b = v_experts[0].T.astype(BF16)
    out = _peer_call(x1.reshape(b * l, d), mod3, row(norm2_g[0]), wpqt, keys, u_b, vt_b, row(final_g),
                     tokens_per_batch=l, tt=min(512, l), et=512)
    return out.reshape(b, l, d)
```

```python
import functools

import jax
import jax.numpy as jnp
import numpy as np
from jax import lax
from jax.experimental import pallas as pl
from jax.experimental.pallas import tpu as pltpu

F32 = jnp.float32
BF16 = jnp.bfloat16

EPS = 1e-6
GRID_W = 64
N_HEADS = 16
NOPE_DIM = 64
ROPE_DIM = 32
V_DIM = 64
ROPE_THETA = 10000.0
ATTN_SCALE = (NOPE_DIM + ROPE_DIM) ** -0.5
LOG2_E = 1.4426950408889634
CONV_WIDTH = 31
CONV_HALF = CONV_WIDTH // 2
P_HEADS = 8
P_TOPK = 16

LANES = 128
SUBLANES = 8
BF16_ROWS = 16
HEAD_PAD = LANES
HALO = 16
R_PER_PIECE = 2
VMEM_LIMIT = 56 * 1024 * 1024

_NT = (((1,), (1,)), ((), ()))


def _dot(a, b):
    return jnp.dot(a, b, preferred_element_type=F32)


def _dot_nt(a, b):
    return lax.dot_general(a, b, _NT, preferred_element_type=F32)


def _rms(x):
    return x * lax.rsqrt(jnp.mean(x * x, axis=-1, keepdims=True) + EPS)


def _cparams(sem):
    return pltpu.CompilerParams(dimension_semantics=sem, vmem_limit_bytes=VMEM_LIMIT)


def _mod_kernel(c_ref, w_ref, b_ref, o_ref):
    cv = c_ref[...]
    s = cv * jax.nn.sigmoid(cv)
    o_ref[...] = jnp.dot(s, w_ref[...], preferred_element_type=F32,
                         precision=lax.Precision.HIGHEST) + b_ref[...]


def _mod_call(cvec, w_mod, b_mod):
    rows, d = cvec.shape
    n = w_mod.shape[1]
    tn = d
    return pl.pallas_call(
        _mod_kernel,
        grid=(n // tn,),
        in_specs=[pl.BlockSpec((rows, d), lambda j: (0, 0)),
                  pl.BlockSpec((d, tn), lambda j: (0, j)),
                  pl.BlockSpec((1, tn), lambda j: (0, j))],
        out_specs=pl.BlockSpec((rows, tn), lambda j: (0, j)),
        out_shape=jax.ShapeDtypeStruct((rows, n), F32),
        compiler_params=_cparams(("arbitrary",)),
        name="mod",
    )(cvec, w_mod, b_mod)


def _modulated_norm(x, g, shift, scale):
    return (_rms(x) * g) * (1.0 + scale) + shift


def _kv_from_h(hb, wkvc_ref, kvg_ref, wukv_ref):
    ckv = _dot(hb, wkvc_ref[...])
    ckvn = (_rms(ckv) * kvg_ref[...]).astype(BF16)
    return _dot(ckvn, wukv_ref[...])


def _inproj_kernel(x_ref, mod_ref, g1_ref, wq_ref, wkvc_ref, wkr_ref, wkrs_ref, wca_ref, wcg_ref,
                   wgate_ref, qg_ref, wuq_ref, wuqs_ref, kvg_ref, wukv_ref,
                   cq_ref, sq_ref, ck_ref, sk_ref,
                   q_out, k_out, v_out, y_out, gate_out):
    d = x_ref.shape[-1]
    hp = N_HEADS * HEAD_PAD
    m = mod_ref[0]
    hb = _modulated_norm(x_ref[0], g1_ref[...], m[:, 0:d], m[:, d:2 * d]).astype(BF16)

    cqn = (_rms(_dot(hb, wq_ref[...])) * qg_ref[...]).astype(BF16)
    qa = _dot(cqn, wuq_ref[...])
    qb = _dot(cqn, wuqs_ref[...])
    cq, sq = cq_ref[...], sq_ref[...]
    for h in range(N_HEADS):
        sl = slice(h * HEAD_PAD, (h + 1) * HEAD_PAD)
        q_out[0, :, sl] = (qa[:, sl] * cq + qb[:, sl] * sq).astype(BF16)

    kv = _kv_from_h(hb, wkvc_ref, kvg_ref, wukv_ref)
    kr = _dot(hb, wkr_ref[...]) * ck_ref[...] + _dot(hb, wkrs_ref[...]) * sk_ref[...]
    for h in range(N_HEADS):
        sl = slice(h * HEAD_PAD, (h + 1) * HEAD_PAD)
        k_out[0, :, sl] = (kv[:, sl] + kr).astype(BF16)
    v_out[0] = kv[:, hp:].astype(BF16)

    a = _dot(hb, wca_ref[...])
    g = _dot(hb, wcg_ref[...])
    y_out[0] = (a * jax.nn.sigmoid(g)).astype(BF16)
    gate_out[0] = jax.nn.sigmoid(_dot(hb, wgate_ref[...])).astype(BF16)


def _const_spec(arr):
    nd = arr.ndim
    return pl.BlockSpec(arr.shape, lambda *_: (0,) * nd)


def _inproj_call(x, mod3, g1, wts, tabs, tm):
    b, l, d = x.shape
    hp = N_HEADS * HEAD_PAD
    nt = l // tm
    weights = [wts[k] for k in ("wq", "wkvc", "wkr", "wkrs", "wca", "wcg", "wgate",
                                "qg", "wuq", "wuqs", "kvg", "wukv")]
    tok = lambda w: pl.BlockSpec((1, tm, w), lambda bi, i: (bi, i, 0))
    tab = pl.BlockSpec((tm, HEAD_PAD), lambda bi, i: (i, 0))
    return pl.pallas_call(
        _inproj_kernel,
        grid=(b, nt),
        in_specs=[tok(d),
                  pl.BlockSpec((1, 1, mod3.shape[-1]), lambda bi, i: (bi, 0, 0)),
                  _const_spec(g1)] + [_const_spec(w) for w in weights] + [tab] * 4,
        out_specs=[tok(hp), tok(hp), tok(hp), tok(d), tok(2 * d)],
        out_shape=[jax.ShapeDtypeStruct((b, l, hp), BF16),
                   jax.ShapeDtypeStruct((b, l, hp), BF16),
                   jax.ShapeDtypeStruct((b, l, hp), BF16),
                   jax.ShapeDtypeStruct((b, l, d), BF16),
                   jax.ShapeDtypeStruct((b, l, 2 * d), BF16)],
        compiler_params=_cparams(("parallel", "parallel")),
        name="in_proj",
    )(x, mod3, g1, *weights, tabs["cq"], tabs["sq"], tabs["ck"], tabs["sk"])


def _ctxkv_kernel(x_ref, mod_ref, g1_ref, wkvc_ref, wkr_ref, kvg_ref, wukv_ref, k_out, v_out):
    d = x_ref.shape[-1]
    hp = N_HEADS * HEAD_PAD
    m = mod_ref[0]
    hb = _modulated_norm(x_ref[0], g1_ref[...], m[:, 0:d], m[:, d:2 * d]).astype(BF16)
    kv = _kv_from_h(hb, wkvc_ref, kvg_ref, wukv_ref)
    kr = _dot(hb, wkr_ref[...])
    for h in range(N_HEADS):
        sl = slice(h * HEAD_PAD, (h + 1) * HEAD_PAD)
        k_out[0, :, sl] = (kv[:, sl] + kr).astype(BF16)
    v_out[0] = kv[:, hp:].astype(BF16)


def _ctxkv_call(ctx, mod3, ctx_row, g1, wts):
    b, n, d = ctx.shape
    hp = N_HEADS * HEAD_PAD
    weights = [wts[k] for k in ("wkvc", "wkr", "kvg", "wukv")]
    tok = lambda w: pl.BlockSpec((1, n, w), lambda bi: (bi, 0, 0))
    return pl.pallas_call(
        _ctxkv_kernel,
        grid=(b,),
        in_specs=[tok(d),
                  pl.BlockSpec((1, 1, mod3.shape[-1]), lambda bi: (ctx_row, 0, 0)),
                  _const_spec(g1)] + [_const_spec(w) for w in weights],
        out_specs=[tok(hp), tok(hp)],
        out_shape=[jax.ShapeDtypeStruct((b, n, hp), BF16)] * 2,
        compiler_params=_cparams(("parallel",)),
        name="ctx_kv",
    )(ctx, mod3, g1, *weights)


def _attn_kernel(q_ref, kl_ref, vl_ref, kc_ref, vc_ref, o_ref, m_ref, l_ref, acc_ref, *, tk):
    n_chunks = kl_ref.shape[1] // tk
    m_ref[...] = jnp.full(m_ref.shape, -jnp.inf, F32)
    l_ref[...] = jnp.zeros(l_ref.shape, F32)
    acc_ref[...] = jnp.zeros(acc_ref.shape, F32)

    def update(head, k, v):
        sl = slice(head * HEAD_PAD, (head + 1) * HEAD_PAD)
        s = _dot_nt(q_ref[0, :, sl], k)
        tiles = [s[:, t * LANES:(t + 1) * LANES] for t in range(s.shape[1] // LANES)]
        tile_max = functools.reduce(jnp.maximum, tiles)
        m_old = m_ref[head]
        m_new = jnp.maximum(m_old, jnp.max(tile_max, axis=-1, keepdims=True))
        alpha = jnp.exp2(m_old - m_new)
        ps = [jnp.exp2(t - m_new) for t in tiles]
        l_ref[head] = alpha * l_ref[head] + functools.reduce(jnp.add, ps)
        p = jnp.concatenate([t.astype(BF16) for t in ps], axis=1)
        acc_ref[head] = alpha * acc_ref[head] + _dot(p, v)
        m_ref[head] = m_new

    def body(c, _):
        off = pl.multiple_of(c * tk, tk)
        for head in range(2):
            sl = slice(head * HEAD_PAD, (head + 1) * HEAD_PAD)
            update(head, kl_ref[0, pl.ds(off, tk), sl], vl_ref[0, pl.ds(off, tk), sl])
        return 0

    lax.fori_loop(0, n_chunks, body, 0)
    out = None
    for head in range(2):
        sl = slice(head * HEAD_PAD, (head + 1) * HEAD_PAD)
        update(head, kc_ref[0, :, sl], vc_ref[0, :, sl])
        o = acc_ref[head] * (1.0 / jnp.sum(l_ref[head], axis=-1, keepdims=True))
        out = o if out is None else out + o
    o_ref[0] = out.astype(BF16)


def _attn_call(q, kl, vl, kc, vc, tq, tk):
    b, l, hp = q.shape
    nc = kc.shape[1]
    pairs = N_HEADS // 2
    pw = 2 * HEAD_PAD
    return pl.pallas_call(
        functools.partial(_attn_kernel, tk=tk),
        grid=(b, pairs, l // tq),
        in_specs=[pl.BlockSpec((1, tq, pw), lambda bi, p, i: (bi, i, p)),
                  pl.BlockSpec((1, l, pw), lambda bi, p, i: (bi, 0, p)),
                  pl.BlockSpec((1, l, pw), lambda bi, p, i: (bi, 0, p)),
                  pl.BlockSpec((1, nc, pw), lambda bi, p, i: (bi, 0, p)),
                  pl.BlockSpec((1, nc, pw), lambda bi, p, i: (bi, 0, p))],
        out_specs=pl.BlockSpec((1, tq, HEAD_PAD), lambda bi, p, i: (bi, i, p)),
        out_shape=jax.ShapeDtypeStruct((b, l, pairs * HEAD_PAD), BF16),
        scratch_shapes=[pltpu.VMEM((2, tq, HEAD_PAD), F32)] * 3,
        compiler_params=_cparams(("parallel", "parallel", "parallel")),
        name="attention",
    )(q, kl, vl, kc, vc)


def _merge_kernel(x_ref, mod_ref, attn_ref, y_ref, yp_ref, yn_ref, gate_ref,
                  womla_ref, wpw_ref, wout_ref, cw_ref, cb_ref, lng_ref, lnb_ref,
                  o_ref, ybuf, sbuf, cbuf, *, rows):
    i = pl.program_id(1)
    last = pl.num_programs(1) - 1
    tm, d = y_ref.shape[1], y_ref.shape[2]

    prev = yp_ref[0].astype(F32)
    nxt = yn_ref[0].astype(F32)
    ybuf[0:HALO, :] = jnp.where(i > 0, prev, 0.0)
    ybuf[HALO:HALO + tm, :] = y_ref[0].astype(F32)
    ybuf[HALO + tm:2 * HALO + tm, :] = jnp.where(i < last, nxt, 0.0)
    span = sbuf.shape[1]
    for p in range(SUBLANES):
        sbuf[p] = ybuf[p:p + span, :]

    cb = cb_ref[...]

    def conv_rows(r, _):
        base = pl.multiple_of(r * rows, rows)
        acc = jnp.broadcast_to(cb, (rows, d))
        for k in range(CONV_WIDTH):
            q, p = divmod(HALO - CONV_HALF + k, SUBLANES)
            acc = acc + cw_ref[k:k + 1, :] * sbuf[p, pl.ds(base + q * SUBLANES, rows), :]
        cbuf[pl.ds(base, rows), :] = acc
        return 0

    lax.fori_loop(0, tm // rows, conv_rows, 0)

    yc = cbuf[...]
    mu = jnp.mean(yc, axis=-1, keepdims=True)
    cen = yc - mu
    var = jnp.mean(cen * cen, axis=-1, keepdims=True)
    ln = cen * lax.rsqrt(var + EPS) * lng_ref[...] + lnb_ref[...]
    act = (ln * jax.nn.sigmoid(ln)).astype(BF16)
    y_c = _dot(act, wpw_ref[...])
    y_a = _dot(attn_ref[0], womla_ref[...])
    gate = gate_ref[0]
    z = gate[:, :d].astype(F32) * y_a + gate[:, d:].astype(F32) * y_c
    zo = _dot(z.astype(BF16), wout_ref[...])
    m = mod_ref[0]
    o_ref[0] = x_ref[0] + m[:, 2 * d:3 * d] * zo


def _merge_call(x, mod3, attn, y, gate, wts, tm):
    b, l, d = x.shape
    nt = l // tm
    hb = tm // HALO
    nhb = l // HALO
    weights = [wts[k] for k in ("womla", "wpw", "wout", "cw", "cb", "lng", "lnb")]
    tok = lambda w: pl.BlockSpec((1, tm, w), lambda bi, i: (bi, i, 0))
    return pl.pallas_call(
        functools.partial(_merge_kernel, rows=32),
        grid=(b, nt),
        in_specs=[tok(d),
                  pl.BlockSpec((1, 1, mod3.shape[-1]), lambda bi, i: (bi, 0, 0)),
                  tok(d), tok(d),
                  pl.BlockSpec((1, HALO, d), lambda bi, i: (bi, jnp.maximum(i * hb - 1, 0), 0)),
                  pl.BlockSpec((1, HALO, d), lambda bi, i: (bi, jnp.minimum((i + 1) * hb, nhb - 1), 0)),
                  tok(2 * d)] + [_const_spec(w) for w in weights],
        out_specs=tok(d),
        out_shape=jax.ShapeDtypeStruct((b, l, d), F32),
        scratch_shapes=[pltpu.VMEM((tm + 2 * HALO, d), F32),
                        pltpu.VMEM((SUBLANES, tm + 2 * HALO - SUBLANES, d), F32),
                        pltpu.VMEM((tm, d), F32)],
        compiler_params=_cparams(("parallel", "parallel")),
        name="merge",
    )(x, mod3, attn, y, y, y, gate, *weights)


def _topk_rank(s, exact):
    nk = s.shape[0]
    key_id = lax.broadcasted_iota(jnp.int32, s.shape, 0)
    rank = jnp.full(s.shape, float(P_TOPK), F32)
    top_id = lax.broadcasted_iota(jnp.int32, (P_TOPK, s.shape[1]), 0)
    tops = jnp.zeros((P_TOPK, s.shape[1]), F32)
    for r in range(P_TOPK):
        cur = jnp.max(s, axis=0, keepdims=True)
        hit = s == cur
        if exact:
            first = jnp.min(jnp.where(hit, key_id, nk), axis=0, keepdims=True)
            hit = key_id == first
        rank = jnp.where(hit, float(r), rank)
        s = jnp.where(hit, -jnp.inf, s)
        tops = jnp.where(top_id == r, cur, tops)
    return tops, rank


def _rank_is_clean(rank):
    kept = jnp.sum(jnp.where(rank < float(P_TOPK), 1.0, 0.0), axis=0, keepdims=True)
    return jnp.max(jnp.abs(kept - float(P_TOPK))) == 0.0


def _staircase(a, b):
    half = P_TOPK // 2
    a_lo, a_hi = a[:half], a[half:]
    row_lo = lax.broadcasted_iota(jnp.int32, a_lo.shape, 0)
    row_hi = row_lo + half
    cnt_lo = jnp.zeros(a_lo.shape, F32)
    cnt_hi = jnp.zeros(a_hi.shape, F32)
    f_lo = a_lo + b[0:1]
    f_hi = a_hi + b[0:1]
    for _ in range(P_TOPK):
        best = jnp.max(jnp.maximum(f_lo, f_hi), axis=0, keepdims=True)
        cand = jnp.minimum(jnp.where(f_lo == best, row_lo, P_TOPK), jnp.where(f_hi == best, row_hi, P_TOPK))
        pick = jnp.min(cand, axis=0, keepdims=True)
        hit_lo = row_lo == pick
        hit_hi = row_hi == pick
        cnt_lo = jnp.where(hit_lo, cnt_lo + 1.0, cnt_lo)
        cnt_hi = jnp.where(hit_hi, cnt_hi + 1.0, cnt_hi)
        nxt = jnp.full(a_lo.shape, -jnp.inf, F32)
        for k in range(1, P_TOPK):
            nxt = jnp.where(cnt_lo == float(k), b[k:k + 1], nxt)
        f_lo = jnp.where(hit_lo, a_lo + nxt, f_lo)
        f_hi = jnp.where(hit_hi, -jnp.inf, f_hi)
    return jnp.concatenate([cnt_lo, cnt_hi], axis=0)


def _peer_kernel(x_ref, mod_ref, g2_ref, wpqt_ref, keys_ref, u_ref, vt_ref, fg_ref, o_ref,
                 h2_ref, sa_ref, rc_ref, top_ref, b2b_ref, rk2b_ref, at0_ref, at1_ref, gt0_ref, gt1_ref,
                 acc_ref):
    s = pl.program_id(1)
    n_steps = pl.num_programs(1)
    tt, d = x_ref.shape
    nlt = tt // LANES
    nk = keys_ref.shape[1]
    et = u_ref.shape[0] // 2
    ke = et // nk
    m = mod_ref[0]

    @pl.when(s == 0)
    def _prepare():
        h2 = _modulated_norm(x_ref[...], g2_ref[...], m[:, 3 * d:4 * d], m[:, 4 * d:5 * d]).astype(BF16)
        h2_ref[...] = h2
        dk = keys_ref.shape[2]
        qt = _dot_nt(wpqt_ref[...], h2).astype(BF16)
        for hp in range(2 * P_HEADS):
            sc = _dot(keys_ref[hp], qt[hp * dk:(hp + 1) * dk, :])
            for lt in range(nlt):
                sa_ref[hp, lt] = sc[:, lt * LANES:(lt + 1) * LANES]

        n_rank = 2 * P_HEADS * nlt

        def rank_units(u, _):
            units = []
            for v in (u, u + n_rank // 2):
                hp, lt = v // nlt, v % nlt
                tops, rank = _topk_rank(sa_ref[hp, lt], exact=False)
                top_ref[hp, lt] = tops
                rc_ref[hp, lt] = rank
                units.append((hp, lt, _rank_is_clean(rank)))
            for hp, lt, clean in units:
                @pl.when(jnp.logical_not(clean))
                def _redo(hp=hp, lt=lt):
                    tops, rank = _topk_rank(sa_ref[hp, lt], exact=True)
                    top_ref[hp, lt] = tops
                    rc_ref[hp, lt] = rank
            return 0

        lax.fori_loop(0, n_rank // 2, rank_units, 0)

        n_gate = P_HEADS * nlt

        def gate_units(u, _):
            for v in (u, u + n_gate // 2):
                h, lt = v // nlt, v % nlt
                a, b = top_ref[2 * h, lt], top_ref[2 * h + 1, lt]
                cnt = _staircase(a, b)
                ea = jnp.exp(a - a[0:1])
                eb = jnp.exp(b - b[0:1])
                cum = jnp.zeros(a.shape, F32)
                for k in range(P_TOPK):
                    cum = cum + jnp.where(cnt > float(k), eb[k:k + 1], 0.0)
                inv_z = 1.0 / jnp.sum(ea * cum, axis=0, keepdims=True)
                r1 = rc_ref[2 * h, lt]
                cnt1 = jnp.zeros(r1.shape, F32)
                for k in range(P_TOPK):
                    cnt1 = jnp.where(r1 == float(k), cnt[k:k + 1], cnt1)
                rc_ref[2 * h, lt] = cnt1
                sa_ref[2 * h, lt] = jnp.exp(sa_ref[2 * h, lt] - a[0:1]) * inv_z
                b2b_ref[h, lt] = jnp.exp(sa_ref[2 * h + 1, lt] - b[0:1]).astype(BF16)
                rk2b_ref[h, lt] = rc_ref[2 * h + 1, lt].astype(BF16)
            return 0

        lax.fori_loop(0, n_gate // 2, gate_units, 0)
        acc_ref[...] = jnp.zeros(acc_ref.shape, F32)
        for ref in (at0_ref, at1_ref, gt0_ref, gt1_ref):
            ref[...] = jnp.zeros(ref.shape, ref.dtype)

    def half_step(half, tile_b, at_src, gt_dst, at_dst, gt_src):
        e1_base = jnp.clip(tile_b * ke, 0, nk - ke)
        groups = nk // BF16_ROWS
        n_trips = ke // R_PER_PIECE
        a_rows = et // n_trips
        c_rows = d // n_trips

        def trip(i, _):
            cr = pl.ds(pl.multiple_of(i * c_rows, c_rows), c_rows)
            acc_ref[cr, :] += _dot(vt_ref[cr, half * et:(half + 1) * et], gt_src[...])
            a_lo = pl.multiple_of(i * a_rows, a_rows)
            u_rows = pl.ds(pl.multiple_of(half * et + a_lo, a_rows), a_rows)
            at_dst[pl.ds(a_lo, a_rows), :] = _dot_nt(u_ref[u_rows, :], h2_ref[...])

            for lt in range(nlt):
                ls = slice(lt * LANES, (lt + 1) * LANES)
                ws = [[jnp.zeros((BF16_ROWS, LANES), BF16) for _ in range(groups)] for _ in range(R_PER_PIECE)]
                for h in range(P_HEADS):
                    b2 = b2b_ref[h, lt]
                    rk2 = rk2b_ref[h, lt]
                    for rr in range(R_PER_PIECE):
                        e1 = e1_base + i * R_PER_PIECE + rr
                        cnt_row = rc_ref[2 * h, lt, pl.ds(e1, 1), :]
                        a_row = sa_ref[2 * h, lt, pl.ds(e1, 1), :]
                        cnt_t = jnp.broadcast_to(cnt_row, (BF16_ROWS, LANES)).astype(BF16)
                        a_t = jnp.broadcast_to(a_row, (BF16_ROWS, LANES)).astype(BF16)
                        for g in range(groups):
                            gs = slice(g * BF16_ROWS, (g + 1) * BF16_ROWS)
                            ws[rr][g] = ws[rr][g] + jnp.where(rk2[gs] < cnt_t, a_t * b2[gs], jnp.zeros_like(a_t))
                for rr in range(R_PER_PIECE):
                    er = pl.ds(pl.multiple_of((i * R_PER_PIECE + rr) * nk, nk), nk)
                    pre = at_src[er, ls]
                    act = 0.5 * pre * (1.0 + lax.erf(pre * np.float32(2.0 ** -0.5)))
                    gt_dst[er, ls] = act.astype(BF16) * jnp.concatenate(ws[rr], axis=0)
            return 0

        lax.fori_loop(0, n_trips, trip, 0)

    half_step(0, 2 * s - 1, at1_ref, gt1_ref, at0_ref, gt0_ref)
    half_step(1, 2 * s, at0_ref, gt0_ref, at1_ref, gt1_ref)

    @pl.when(s == n_steps - 1)
    def _finish():
        x2 = x_ref[...] + m[:, 5 * d:6 * d] * acc_ref[...].T
        o_ref[...] = _rms(x2) * fg_ref[...]


def _peer_call(x1, mod3, g2, wpqt, keys, u_b, vt_b, fg, tokens_per_batch, tt, et):
    n, d = x1.shape
    ne = u_b.shape[0]
    nk = keys.shape[1]
    nlt = tt // LANES
    n_pairs = ne // (2 * et)
    tiles_per_batch = tokens_per_batch // tt
    unit = (2 * P_HEADS, nlt, nk, LANES)
    return pl.pallas_call(
        _peer_kernel,
        grid=(n // tt, n_pairs + 1),
        in_specs=[pl.BlockSpec((tt, d), lambda t, j: (t, 0)),
                  pl.BlockSpec((1, 1, mod3.shape[-1]), lambda t, j: (t // tiles_per_batch, 0, 0)),
                  _const_spec(g2), _const_spec(wpqt), _const_spec(keys),
                  pl.BlockSpec((2 * et, d), lambda t, j: (jnp.minimum(j, n_pairs - 1), 0)),
                  pl.BlockSpec((d, 2 * et), lambda t, j: (0, jnp.maximum(j - 1, 0))),
                  _const_spec(fg)],
        out_specs=pl.BlockSpec((tt, d), lambda t, j: (t, 0)),
        out_shape=jax.ShapeDtypeStruct((n, d), F32),
        scratch_shapes=[pltpu.VMEM((tt, d), BF16),
                        pltpu.VMEM(unit, F32),
                        pltpu.VMEM(unit, F32),
                        pltpu.VMEM((2 * P_HEADS, nlt, P_TOPK, LANES), F32),
                        pltpu.VMEM((P_HEADS, nlt, nk, LANES), BF16),
                        pltpu.VMEM((P_HEADS, nlt, nk, LANES), BF16),
                        pltpu.VMEM((et, tt), F32), pltpu.VMEM((et, tt), F32),
                        pltpu.VMEM((et, tt), BF16), pltpu.VMEM((et, tt), BF16),
                        pltpu.VMEM((d, tt), F32)],
        compiler_params=_cparams(("parallel", "arbitrary")),
        name="peer",
    )(x1, mod3, g2, wpqt, keys, u_b, vt_b, fg)


def _rope_tables(n_tok):
    rows = n_tok // GRID_W
    r = jnp.repeat(jnp.arange(rows, dtype=jnp.int32), GRID_W)
    col = jnp.tile(jnp.arange(GRID_W, dtype=jnp.int32), rows)
    n_freq = ROPE_DIM // 4
    freq = ROPE_THETA ** (-jnp.arange(n_freq, dtype=F32) / n_freq)
    ang = jnp.stack([r[:, None] * freq, col[:, None] * freq], axis=1)
    dd = np.arange(ROPE_DIM)
    axis, half, fr = dd // (ROPE_DIM // 2), (dd % (ROPE_DIM // 2)) // n_freq, dd % n_freq
    c32 = jnp.cos(ang)[:, axis, fr]
    s32 = jnp.sin(ang)[:, axis, fr] * jnp.where(half == 0, -1.0, 1.0).astype(F32)
    pad = jnp.zeros((n_tok, HEAD_PAD - NOPE_DIM - ROPE_DIM), F32)
    ones = jnp.ones((n_tok, NOPE_DIM), F32)
    ck = jnp.concatenate([ones, c32, pad], axis=1)
    sk = jnp.concatenate([0.0 * ones, s32, pad], axis=1)
    q_scale = ATTN_SCALE * LOG2_E
    return {"cq": ck * q_scale, "sq": sk * q_scale, "ck": ck, "sk": sk}


def _prep_weights(w_in, q_norm_g, w_uq, kv_norm_g, w_ukv, w_o_mla, conv_w, conv_b, conv_ln_g,
                  conv_ln_b, w_pw, w_out):
    d = w_in.shape[0]
    q_rank = q_norm_g.shape[0]
    kv_rank = kv_norm_g.shape[0]
    off_kv = q_rank
    off_kr = q_rank + kv_rank
    off_conv = off_kr + ROPE_DIM
    off_gate = off_conv + 2 * d
    qk = NOPE_DIM + ROPE_DIM
    tail = HEAD_PAD - qk

    def lane_pad(w, before, after):
        return jnp.pad(w, [(0, 0)] * (w.ndim - 1) + [(before, after)])

    def partner(w):
        g = w.reshape(w.shape[:-1] + (2, 2, ROPE_DIM // 4))
        return g[..., ::-1, :].reshape(w.shape)

    def place_rope(w32):
        return lane_pad(w32, NOPE_DIM, tail)

    wkr32 = w_in[:, off_kr:off_conv]
    uq = w_uq.reshape(q_rank, N_HEADS, qk)
    uq_pad = lane_pad(uq, 0, tail)
    uq_sw = lane_pad(partner(uq[:, :, NOPE_DIM:]), NOPE_DIM, tail)
    ukv = w_ukv.reshape(kv_rank, N_HEADS, NOPE_DIM + V_DIM)
    uk_pad = lane_pad(ukv[:, :, :NOPE_DIM], 0, HEAD_PAD - NOPE_DIM)
    uv = ukv[:, :, NOPE_DIM:]
    even = (np.arange(N_HEADS) % 2 == 0)[None, :, None]
    uv_pad = jnp.concatenate([jnp.where(even, uv, 0.0), jnp.where(even, 0.0, uv)], axis=-1)
    hp = N_HEADS * HEAD_PAD
    row = lambda v: v.reshape(1, -1).astype(F32)
    return {
        "wq": w_in[:, :off_kv].astype(BF16),
        "wkvc": w_in[:, off_kv:off_kr].astype(BF16),
        "wkr": place_rope(wkr32).astype(BF16),
        "wkrs": place_rope(partner(wkr32)).astype(BF16),
        "wca": w_in[:, off_conv:off_conv + d].astype(BF16),
        "wcg": w_in[:, off_conv + d:off_gate].astype(BF16),
        "wgate": w_in[:, off_gate:].astype(BF16),
        "qg": row(q_norm_g),
        "wuq": uq_pad.reshape(q_rank, hp).astype(BF16),
        "wuqs": uq_sw.reshape(q_rank, hp).astype(BF16),
        "kvg": row(kv_norm_g),
        "wukv": jnp.concatenate([uk_pad.reshape(kv_rank, hp), uv_pad.reshape(kv_rank, hp)], axis=1).astype(BF16),
        "womla": w_o_mla.astype(BF16),
        "wpw": w_pw.astype(BF16),
        "wout": w_out.astype(BF16),
        "cw": conv_w.astype(F32),
        "cb": row(conv_b),
        "lng": row(conv_ln_g),
        "lnb": row(conv_ln_b),
    }


def kernel(x, c, ctx, c_ctx, w_mod, b_mod, norm1_g, norm2_g, w_in, q_norm_g, w_uq, kv_norm_g, w_ukv,
           w_o_mla, conv_w, conv_b, conv_ln_g, conv_ln_b, w_pw, w_out, w_pq, sub_keys, u_experts,
           v_experts, final_g):
    assert w_mod.shape[0] == 1, "single-layer configuration"
    b, l, d = x.shape
    assert c.shape[0] + 1 <= 8
    row = lambda v: v.reshape(1, -1).astype(F32)

    cvec = jnp.concatenate([c, c_ctx[None, :], jnp.zeros((SUBLANES - b - 1, d), F32)], axis=0)
    mod = _mod_call(cvec, w_mod[0], b_mod[0].reshape(1, -1))
    mod3 = mod.reshape(8, 1, 6 * d)

    wts = _prep_weights(w_in[0], q_norm_g[0], w_uq[0], kv_norm_g[0], w_ukv[0], w_o_mla[0], conv_w[0],
                        conv_b[0], conv_ln_g[0], conv_ln_b[0], w_pw[0], w_out[0])
    tabs = _rope_tables(l)
    g1 = row(norm1_g[0])

    tm = min(256, l)
    q, kl, vl, y, gate = _inproj_call(x, mod3, g1, wts, tabs, tm)
    kc, vc = _ctxkv_call(ctx, mod3, b, g1, wts)
    attn = _attn_call(q, kl, vl, kc, vc, tq=min(512, l), tk=min(2048, l))
    x1 = _merge_call(x, mod3, attn, y, gate, wts, tm=min(256, l))

    n_keys = sub_keys.shape[3]
    keys = sub_keys[0].reshape(2 * P_HEADS, n_keys, -1).astype(BF16)
    wpqt = w_pq[0].T.astype(BF16)
    u_b = u_experts[0].astype(BF16)
    vt_b = v_experts[0].T.astype(BF16)
    out = _peer_call(x1.reshape(b * l, d), mod3, row(norm2_g[0]), wpqt, keys, u_b, vt_b, row(final_g),
                     tokens_per_batch=l, tt=min(512, l), et=512)
    return out.reshape(b, l, d)
```

```python
import functools

import jax
import jax.numpy as jnp
import numpy as np
from jax import lax
from jax.experimental import pallas as pl
from jax.experimental.pallas import tpu as pltpu

F32 = jnp.float32
BF16 = jnp.bfloat16

EPS = 1e-6
GRID_W = 64
N_HEADS = 16
NOPE_DIM = 64
ROPE_DIM = 32
V_DIM = 64
ROPE_THETA = 10000.0
ATTN_SCALE = (NOPE_DIM + ROPE_DIM) ** -0.5
LOG2_E = 1.4426950408889634
CONV_WIDTH = 31
CONV_HALF = CONV_WIDTH // 2
P_HEADS = 8
P_TOPK = 16

LANES = 128
SUBLANES = 8
BF16_ROWS = 16
HEAD_PAD = LANES
HALO = 16
R_PER_PIECE = 2
VMEM_LIMIT = 56 * 1024 * 1024

_NT = (((1,), (1,)), ((), ()))


def _dot(a, b):
    return jnp.dot(a, b, preferred_element_type=F32)


def _dot_nt(a, b):
    return lax.dot_general(a, b, _NT, preferred_element_type=F32)


def _rms(x):
    return x * lax.rsqrt(jnp.mean(x * x, axis=-1, keepdims=True) + EPS)


def _cparams(sem):
    return pltpu.CompilerParams(dimension_semantics=sem, vmem_limit_bytes=VMEM_LIMIT)


def _mod_kernel(c_ref, w_ref, b_ref, o_ref):
    cv = c_ref[...]
    s = cv * jax.nn.sigmoid(cv)
    o_ref[...] = jnp.dot(s, w_ref[...], preferred_element_type=F32,
                         precision=lax.Precision.HIGHEST) + b_ref[...]


def _mod_call(cvec, w_mod, b_mod):
    rows, d = cvec.shape
    n = w_mod.shape[1]
    tn = d
    return pl.pallas_call(
        _mod_kernel,
        grid=(n // tn,),
        in_specs=[pl.BlockSpec((rows, d), lambda j: (0, 0)),
                  pl.BlockSpec((d, tn), lambda j: (0, j)),
                  pl.BlockSpec((1, tn), lambda j: (0, j))],
        out_specs=pl.BlockSpec((rows, tn), lambda j: (0, j)),
        out_shape=jax.ShapeDtypeStruct((rows, n), F32),
        compiler_params=_cparams(("arbitrary",)),
        name="mod",
    )(cvec, w_mod, b_mod)


def _modulated_norm(x, g, shift, scale):
    return (_rms(x) * g) * (1.0 + scale) + shift


def _kv_from_h(hb, wkvc_ref, kvg_ref, wukv_ref):
    ckv = _dot(hb, wkvc_ref[...])
    ckvn = (_rms(ckv) * kvg_ref[...]).astype(BF16)
    return _dot(ckvn, wukv_ref[...])


def _inproj_kernel(x_ref, mod_ref, g1_ref, wq_ref, wkvc_ref, wkr_ref, wkrs_ref, wca_ref, wcg_ref,
                   wgate_ref, qg_ref, wuq_ref, wuqs_ref, kvg_ref, wukv_ref,
                   cq_ref, sq_ref, ck_ref, sk_ref,
                   q_out, k_out, v_out, y_out, gate_out):
    d = x_ref.shape[-1]
    hp = N_HEADS * HEAD_PAD
    m = mod_ref[0]
    hb = _modulated_norm(x_ref[0], g1_ref[...], m[:, 0:d], m[:, d:2 * d]).astype(BF16)

    cqn = (_rms(_dot(hb, wq_ref[...])) * qg_ref[...]).astype(BF16)
    qa = _dot(cqn, wuq_ref[...])
    qb = _dot(cqn, wuqs_ref[...])
    cq, sq = cq_ref[...], sq_ref[...]
    for h in range(N_HEADS):
        sl = slice(h * HEAD_PAD, (h + 1) * HEAD_PAD)
        q_out[0, :, sl] = (qa[:, sl] * cq + qb[:, sl] * sq).astype(BF16)

    kv = _kv_from_h(hb, wkvc_ref, kvg_ref, wukv_ref)
    kr = _dot(hb, wkr_ref[...]) * ck_ref[...] + _dot(hb, wkrs_ref[...]) * sk_ref[...]
    for h in range(N_HEADS):
        sl = slice(h * HEAD_PAD, (h + 1) * HEAD_PAD)
        k_out[0, :, sl] = (kv[:, sl] + kr).astype(BF16)
    v_out[0] = kv[:, hp:].astype(BF16)

    a = _dot(hb, wca_ref[...])
    g = _dot(hb, wcg_ref[...])
    y_out[0] = (a * jax.nn.sigmoid(g)).astype(BF16)
    gate_out[0] = jax.nn.sigmoid(_dot(hb, wgate_ref[...])).astype(BF16)


def _const_spec(arr):
    nd = arr.ndim
    return pl.BlockSpec(arr.shape, lambda *_: (0,) * nd)


def _inproj_call(x, mod3, g1, wts, tabs, tm):
    b, l, d = x.shape
    hp = N_HEADS * HEAD_PAD
    nt = l // tm
    weights = [wts[k] for k in ("wq", "wkvc", "wkr", "wkrs", "wca", "wcg", "wgate",
                                "qg", "wuq", "wuqs", "kvg", "wukv")]
    tok = lambda w: pl.BlockSpec((1, tm, w), lambda bi, i: (bi, i, 0))
    tab = pl.BlockSpec((tm, HEAD_PAD), lambda bi, i: (i, 0))
    return pl.pallas_call(
        _inproj_kernel,
        grid=(b, nt),
        in_specs=[tok(d),
                  pl.BlockSpec((1, 1, mod3.shape[-1]), lambda bi, i: (bi, 0, 0)),
                  _const_spec(g1)] + [_const_spec(w) for w in weights] + [tab] * 4,
        out_specs=[tok(hp), tok(hp), tok(hp), tok(d), tok(2 * d)],
        out_shape=[jax.ShapeDtypeStruct((b, l, hp), BF16),
                   jax.ShapeDtypeStruct((b, l, hp), BF16),
                   jax.ShapeDtypeStruct((b, l, hp), BF16),
                   jax.ShapeDtypeStruct((b, l, d), BF16),
                   jax.ShapeDtypeStruct((b, l, 2 * d), BF16)],
        compiler_params=_cparams(("parallel", "parallel")),
        name="in_proj",
    )(x, mod3, g1, *weights, tabs["cq"], tabs["sq"], tabs["ck"], tabs["sk"])


def _ctxkv_kernel(x_ref, mod_ref, g1_ref, wkvc_ref, wkr_ref, kvg_ref, wukv_ref, k_out, v_out):
    d = x_ref.shape[-1]
    hp = N_HEADS * HEAD_PAD
    m = mod_ref[0]
    hb = _modulated_norm(x_ref[0], g1_ref[...], m[:, 0:d], m[:, d:2 * d]).astype(BF16)
    kv = _kv_from_h(hb, wkvc_ref, kvg_ref, wukv_ref)
    kr = _dot(hb, wkr_ref[...])
    for h in range(N_HEADS):
        sl = slice(h * HEAD_PAD, (h + 1) * HEAD_PAD)
        k_out[0, :, sl] = (kv[:, sl] + kr).astype(BF16)
    v_out[0] = kv[:, hp:].astype(BF16)


def _ctxkv_call(ctx, mod3, ctx_row, g1, wts):
    b, n, d = ctx.shape
    hp = N_HEADS * HEAD_PAD
    weights = [wts[k] for k in ("wkvc", "wkr", "kvg", "wukv")]
    tok = lambda w: pl.BlockSpec((1, n, w), lambda bi: (bi, 0, 0))
    return pl.pallas_call(
        _ctxkv_kernel,
        grid=(b,),
        in_specs=[tok(d),
                  pl.BlockSpec((1, 1, mod3.shape[-1]), lambda bi: (ctx_row, 0, 0)),
                  _const_spec(g1)] + [_const_spec(w) for w in weights],
        out_specs=[tok(hp), tok(hp)],
        out_shape=[jax.ShapeDtypeStruct((b, n, hp), BF16)] * 2,
        compiler_params=_cparams(("parallel",)),
        name="ctx_kv",
    )(ctx, mod3, g1, *weights)


def _attn_kernel(q_ref, kl_ref, vl_ref, kc_ref, vc_ref, o_ref, m_ref, l_ref, acc_ref, *, tk):
    n_chunks = kl_ref.shape[1] // tk
    m_ref[...] = jnp.full(m_ref.shape, -jnp.inf, F32)
    l_ref[...] = jnp.zeros(l_ref.shape, F32)
    acc_ref[...] = jnp.zeros(acc_ref.shape, F32)

    def update(head, k, v):
        sl = slice(head * HEAD_PAD, (head + 1) * HEAD_PAD)
        s = _dot_nt(q_ref[0, :, sl], k)
        tiles = [s[:, t * LANES:(t + 1) * LANES] for t in range(s.shape[1] // LANES)]
        tile_max = functools.reduce(jnp.maximum, tiles)
        m_old = m_ref[head]
        m_new = jnp.maximum(m_old, jnp.max(tile_max, axis=-1, keepdims=True))
        alpha = jnp.exp2(m_old - m_new)
        ps = [jnp.exp2(t - m_new) for t in tiles]
        l_ref[head] = alpha * l_ref[head] + functools.reduce(jnp.add, ps)
        p = jnp.concatenate([t.astype(BF16) for t in ps], axis=1)
        acc_ref[head] = alpha * acc_ref[head] + _dot(p, v)
        m_ref[head] = m_new

    def body(c, _):
        off = pl.multiple_of(c * tk, tk)
        for head in range(2):
            sl = slice(head * HEAD_PAD, (head + 1) * HEAD_PAD)
            update(head, kl_ref[0, pl.ds(off, tk), sl], vl_ref[0, pl.ds(off, tk), sl])
        return 0

    lax.fori_loop(0, n_chunks, body, 0)
    out = None
    for head in range(2):
        sl = slice(head * HEAD_PAD, (head + 1) * HEAD_PAD)
        update(head, kc_ref[0, :, sl], vc_ref[0, :, sl])
        o = acc_ref[head] * (1.0 / jnp.sum(l_ref[head], axis=-1, keepdims=True))
        out = o if out is None else out + o
    o_ref[0] = out.astype(BF16)


def _attn_call(q, kl, vl, kc, vc, tq, tk):
    b, l, hp = q.shape
    nc = kc.shape[1]
    pairs = N_HEADS // 2
    pw = 2 * HEAD_PAD
    return pl.pallas_call(
        functools.partial(_attn_kernel, tk=tk),
        grid=(b, pairs, l // tq),
        in_specs=[pl.BlockSpec((1, tq, pw), lambda bi, p, i: (bi, i, p)),
                  pl.BlockSpec((1, l, pw), lambda bi, p, i: (bi, 0, p)),
                  pl.BlockSpec((1, l, pw), lambda bi, p, i: (bi, 0, p)),
                  pl.BlockSpec((1, nc, pw), lambda bi, p, i: (bi, 0, p)),
                  pl.BlockSpec((1, nc, pw), lambda bi, p, i: (bi, 0, p))],
        out_specs=pl.BlockSpec((1, tq, HEAD_PAD), lambda bi, p, i: (bi, i, p)),
        out_shape=jax.ShapeDtypeStruct((b, l, pairs * HEAD_PAD), BF16),
        scratch_shapes=[pltpu.VMEM((2, tq, HEAD_PAD), F32)] * 3,
        compiler_params=_cparams(("parallel", "parallel", "parallel")),
        name="attention",
    )(q, kl, vl, kc, vc)


def _merge_kernel(x_ref, mod_ref, attn_ref, y_ref, yp_ref, yn_ref, gate_ref,
                  womla_ref, wpw_ref, wout_ref, cw_ref, cb_ref, lng_ref, lnb_ref,
                  o_ref, ybuf, sbuf, cbuf, *, rows):
    i = pl.program_id(1)
    last = pl.num_programs(1) - 1
    tm, d = y_ref.shape[1], y_ref.shape[2]

    prev = yp_ref[0].astype(F32)
    nxt = yn_ref[0].astype(F32)
    ybuf[0:HALO, :] = jnp.where(i > 0, prev, 0.0)
    ybuf[HALO:HALO + tm, :] = y_ref[0].astype(F32)
    ybuf[HALO + tm:2 * HALO + tm, :] = jnp.where(i < last, nxt, 0.0)
    span = sbuf.shape[1]
    for p in range(SUBLANES):
        sbuf[p] = ybuf[p:p + span, :]

    cb = cb_ref[...]

    def conv_rows(r, _):
        base = pl.multiple_of(r * rows, rows)
        acc = jnp.broadcast_to(cb, (rows, d))
        for k in range(CONV_WIDTH):
            q, p = divmod(HALO - CONV_HALF + k, SUBLANES)
            acc = acc + cw_ref[k:k + 1, :] * sbuf[p, pl.ds(base + q * SUBLANES, rows), :]
        cbuf[pl.ds(base, rows), :] = acc
        return 0

    lax.fori_loop(0, tm // rows, conv_rows, 0)

    yc = cbuf[...]
    mu = jnp.mean(yc, axis=-1, keepdims=True)
    cen = yc - mu
    var = jnp.mean(cen * cen, axis=-1, keepdims=True)
    ln = cen * lax.rsqrt(var + EPS) * lng_ref[...] + lnb_ref[...]
    act = (ln * jax.nn.sigmoid(ln)).astype(BF16)
    y_c = _dot(act, wpw_ref[...])
    y_a = _dot(attn_ref[0], womla_ref[...])
    gate = gate_ref[0]
    z = gate[:, :d].astype(F32) * y_a + gate[:, d:].astype(F32) * y_c
    zo = _dot(z.astype(BF16), wout_ref[...])
    m = mod_ref[0]
    o_ref[0] = x_ref[0] + m[:, 2 * d:3 * d] * zo


def _merge_call(x, mod3, attn, y, gate, wts, tm):
    b, l, d = x.shape
    nt = l // tm
    hb = tm // HALO
    nhb = l // HALO
    weights = [wts[k] for k in ("womla", "wpw", "wout", "cw", "cb", "lng", "lnb")]
    tok = lambda w: pl.BlockSpec((1, tm, w), lambda bi, i: (bi, i, 0))
    return pl.pallas_call(
        functools.partial(_merge_kernel, rows=32),
        grid=(b, nt),
        in_specs=[tok(d),
                  pl.BlockSpec((1, 1, mod3.shape[-1]), lambda bi, i: (bi, 0, 0)),
                  tok(d), tok(d),
                  pl.BlockSpec((1, HALO, d), lambda bi, i: (bi, jnp.maximum(i * hb - 1, 0), 0)),
                  pl.BlockSpec((1, HALO, d), lambda bi, i: (bi, jnp.minimum((i + 1) * hb, nhb - 1), 0)),
                  tok(2 * d)] + [_const_spec(w) for w in weights],
        out_specs=tok(d),
        out_shape=jax.ShapeDtypeStruct((b, l, d), F32),
        scratch_shapes=[pltpu.VMEM((tm + 2 * HALO, d), F32),
                        pltpu.VMEM((SUBLANES, tm + 2 * HALO - SUBLANES, d), F32),
                        pltpu.VMEM((tm, d), F32)],
        compiler_params=_cparams(("parallel", "parallel")),
        name="merge",
    )(x, mod3, attn, y, y, y, gate, *weights)


def _topk_rank(s, exact):
    nk = s.shape[0]
    key_id = lax.broadcasted_iota(jnp.int32, s.shape, 0)
    rank = jnp.full(s.shape, float(P_TOPK), F32)
    top_id = lax.broadcasted_iota(jnp.int32, (P_TOPK, s.shape[1]), 0)
    tops = jnp.zeros((P_TOPK, s.shape[1]), F32)
    for r in range(P_TOPK):
        cur = jnp.max(s, axis=0, keepdims=True)
        hit = s == cur
        if exact:
            first = jnp.min(jnp.where(hit, key_id, nk), axis=0, keepdims=True)
            hit = key_id == first
        rank = jnp.where(hit, float(r), rank)
        s = jnp.where(hit, -jnp.inf, s)
        tops = jnp.where(top_id == r, cur, tops)
    return tops, rank


def _rank_is_clean(rank):
    kept = jnp.sum(jnp.where(rank < float(P_TOPK), 1.0, 0.0), axis=0, keepdims=True)
    return jnp.max(jnp.abs(kept - float(P_TOPK))) == 0.0


def _staircase(a, b):
    half = P_TOPK // 2
    a_lo, a_hi = a[:half], a[half:]
    row_lo = lax.broadcasted_iota(jnp.int32, a_lo.shape, 0)
    row_hi = row_lo + half
    cnt_lo = jnp.zeros(a_lo.shape, F32)
    cnt_hi = jnp.zeros(a_hi.shape, F32)
    f_lo = a_lo + b[0:1]
    f_hi = a_hi + b[0:1]
    for _ in range(P_TOPK):
        best = jnp.max(jnp.maximum(f_lo, f_hi), axis=0, keepdims=True)
        cand = jnp.minimum(jnp.where(f_lo == best, row_lo, P_TOPK), jnp.where(f_hi == best, row_hi, P_TOPK))
        pick = jnp.min(cand, axis=0, keepdims=True)
        hit_lo = row_lo == pick
        hit_hi = row_hi == pick
        cnt_lo = jnp.where(hit_lo, cnt_lo + 1.0, cnt_lo)
        cnt_hi = jnp.where(hit_hi, cnt_hi + 1.0, cnt_hi)
        nxt = jnp.full(a_lo.shape, -jnp.inf, F32)
        for k in range(1, P_TOPK):
            nxt = jnp.where(cnt_lo == float(k), b[k:k + 1], nxt)
        f_lo = jnp.where(hit_lo, a_lo + nxt, f_lo)
        f_hi = jnp.where(hit_hi, -jnp.inf, f_hi)
    return jnp.concatenate([cnt_lo, cnt_hi], axis=0)


def _peer_kernel(x_ref, mod_ref, g2_ref, wpqt_ref, keys_ref, u_ref, vt_ref, fg_ref, o_ref,
                 h2_ref, sa_ref, rc_ref, top_ref, b2b_ref, rk2b_ref, at0_ref, at1_ref, gt0_ref, gt1_ref,
                 acc_ref):
    s = pl.program_id(1)
    n_steps = pl.num_programs(1)
    tt, d = x_ref.shape
    nlt = tt // LANES
    nk = keys_ref.shape[1]
    et = u_ref.shape[0] // 2
    ke = et // nk
    m = mod_ref[0]

    @pl.when(s == 0)
    def _prepare():
        h2 = _modulated_norm(x_ref[...], g2_ref[...], m[:, 3 * d:4 * d], m[:, 4 * d:5 * d])
        h2_ref[...] = h2.T.astype(BF16)
        dk = keys_ref.shape[2]
        qt = _dot(wpqt_ref[...], h2_ref[...]).astype(BF16)
        for hp in range(2 * P_HEADS):
            sc = _dot(keys_ref[hp], qt[hp * dk:(hp + 1) * dk, :])
            for lt in range(nlt):
                sa_ref[hp, lt] = sc[:, lt * LANES:(lt + 1) * LANES]

        n_rank = 2 * P_HEADS * nlt

        def rank_units(u, _):
            units = []
            for v in (u, u + n_rank // 2):
                hp, lt = v // nlt, v % nlt
                tops, rank = _topk_rank(sa_ref[hp, lt], exact=False)
                top_ref[hp, lt] = tops
                rc_ref[hp, lt] = rank
                units.append((hp, lt, _rank_is_clean(rank)))
            for hp, lt, clean in units:
                @pl.when(jnp.logical_not(clean))
                def _redo(hp=hp, lt=lt):
                    tops, rank = _topk_rank(sa_ref[hp, lt], exact=True)
                    top_ref[hp, lt] = tops
                    rc_ref[hp, lt] = rank
            return 0

        lax.fori_loop(0, n_rank // 2, rank_units, 0)

        n_gate = P_HEADS * nlt

        def gate_units(u, _):
            for v in (u, u + n_gate // 2):
                h, lt = v // nlt, v % nlt
                a, b = top_ref[2 * h, lt], top_ref[2 * h + 1, lt]
                cnt = _staircase(a, b)
                ea = jnp.exp(a - a[0:1])
                eb = jnp.exp(b - b[0:1])
                cum = jnp.zeros(a.shape, F32)
                for k in range(P_TOPK):
                    cum = cum + jnp.where(cnt > float(k), eb[k:k + 1], 0.0)
                inv_z = 1.0 / jnp.sum(ea * cum, axis=0, keepdims=True)
                r1 = rc_ref[2 * h, lt]
                cnt1 = jnp.zeros(r1.shape, F32)
                for k in range(P_TOPK):
                    cnt1 = jnp.where(r1 == float(k), cnt[k:k + 1], cnt1)
                rc_ref[2 * h, lt] = cnt1
                sa_ref[2 * h, lt] = jnp.exp(sa_ref[2 * h, lt] - a[0:1]) * inv_z
                b2b_ref[h, lt] = jnp.exp(sa_ref[2 * h + 1, lt] - b[0:1]).astype(BF16)
                rk2b_ref[h, lt] = rc_ref[2 * h + 1, lt].astype(BF16)
            return 0

        lax.fori_loop(0, n_gate // 2, gate_units, 0)
        acc_ref[...] = jnp.zeros(acc_ref.shape, F32)
        for ref in (at0_ref, at1_ref, gt0_ref, gt1_ref):
            ref[...] = jnp.zeros(ref.shape, ref.dtype)

    def half_step(half, tile_b, at_src, gt_dst, at_dst, gt_src):
        e1_base = jnp.clip(tile_b * ke, 0, nk - ke)
        groups = nk // BF16_ROWS
        n_trips = ke // R_PER_PIECE
        a_rows = et // n_trips
        c_rows = d // n_trips

        def trip(i, _):
            cr = pl.ds(pl.multiple_of(i * c_rows, c_rows), c_rows)
            acc_ref[cr, :] += _dot(vt_ref[cr, half * et:(half + 1) * et], gt_src[...])
            a_lo = pl.multiple_of(i * a_rows, a_rows)
            u_rows = pl.ds(pl.multiple_of(half * et + a_lo, a_rows), a_rows)
            at_dst[pl.ds(a_lo, a_rows), :] = _dot(u_ref[u_rows, :], h2_ref[...])

            for lt in range(nlt):
                ls = slice(lt * LANES, (lt + 1) * LANES)
                ws = [[jnp.zeros((BF16_ROWS, LANES), BF16) for _ in range(groups)] for _ in range(R_PER_PIECE)]
                for h in range(P_HEADS):
                    b2 = b2b_ref[h, lt]
                    rk2 = rk2b_ref[h, lt]
                    for rr in range(R_PER_PIECE):
                        e1 = e1_base + i * R_PER_PIECE + rr
                        cnt_row = rc_ref[2 * h, lt, pl.ds(e1, 1), :]
                        a_row = sa_ref[2 * h, lt, pl.ds(e1, 1), :]
                        cnt_t = jnp.broadcast_to(cnt_row, (BF16_ROWS, LANES)).astype(BF16)
                        a_t = jnp.broadcast_to(a_row, (BF16_ROWS, LANES)).astype(BF16)
                        for g in range(groups):
                            gs = slice(g * BF16_ROWS, (g + 1) * BF16_ROWS)
                            ws[rr][g] = ws[rr][g] + a_t * jnp.where(rk2[gs] < cnt_t, b2[gs], jnp.zeros_like(a_t))
                for rr in range(R_PER_PIECE):
                    er = pl.ds(pl.multiple_of((i * R_PER_PIECE + rr) * nk, nk), nk)
                    pre = at_src[er, ls]
                    act = 0.5 * pre * (1.0 + lax.erf(pre * np.float32(2.0 ** -0.5)))
                    gt_dst[er, ls] = act.astype(BF16) * jnp.concatenate(ws[rr], axis=0)
            return 0

        lax.fori_loop(0, n_trips, trip, 0)

    half_step(0, 2 * s - 1, at1_ref, gt1_ref, at0_ref, gt0_ref)
    half_step(1, 2 * s, at0_ref, gt0_ref, at1_ref, gt1_ref)

    @pl.when(s == n_steps - 1)
    def _finish():
        x2 = x_ref[...] + m[:, 5 * d:6 * d] * acc_ref[...].T
        o_ref[...] = _rms(x2) * fg_ref[...]


def _peer_call(x1, mod3, g2, wpqt, keys, u_b, vt_b, fg, tokens_per_batch, tt, et):
    n, d = x1.shape
    ne = u_b.shape[0]
    nk = keys.shape[1]
    nlt = tt // LANES
    n_pairs = ne // (2 * et)
    tiles_per_batch = tokens_per_batch // tt
    unit = (2 * P_HEADS, nlt, nk, LANES)
    return pl.pallas_call(
        _peer_kernel,
        grid=(n // tt, n_pairs + 1),
        in_specs=[pl.BlockSpec((tt, d), lambda t, j: (t, 0)),
                  pl.BlockSpec((1, 1, mod3.shape[-1]), lambda t, j: (t // tiles_per_batch, 0, 0)),
                  _const_spec(g2), _const_spec(wpqt), _const_spec(keys),
                  pl.BlockSpec((2 * et, d), lambda t, j: (jnp.minimum(j, n_pairs - 1), 0)),
                  pl.BlockSpec((d, 2 * et), lambda t, j: (0, jnp.maximum(j - 1, 0))),
                  _const_spec(fg)],
        out_specs=pl.BlockSpec((tt, d), lambda t, j: (t, 0)),
        out_shape=jax.ShapeDtypeStruct((n, d), F32),
        scratch_shapes=[pltpu.VMEM((d, tt), BF16),
                        pltpu.VMEM(unit, F32),
                        pltpu.VMEM(unit, F32),
                        pltpu.VMEM((2 * P_HEADS, nlt, P_TOPK, LANES), F32),
                        pltpu.VMEM((P_HEADS, nlt, nk, LANES), BF16),
                        pltpu.VMEM((P_HEADS, nlt, nk, LANES), BF16),
                        pltpu.VMEM((et, tt), F32), pltpu.VMEM((et, tt), F32),
                        pltpu.VMEM((et, tt), BF16), pltpu.VMEM((et, tt), BF16),
                        pltpu.VMEM((d, tt), F32)],
        compiler_params=_cparams(("parallel", "arbitrary")),
        name="peer",
    )(x1, mod3, g2, wpqt, keys, u_b, vt_b, fg)


def _rope_tables(n_tok):
    rows = n_tok // GRID_W
    r = jnp.repeat(jnp.arange(rows, dtype=jnp.int32), GRID_W)
    col = jnp.tile(jnp.arange(GRID_W, dtype=jnp.int32), rows)
    n_freq = ROPE_DIM // 4
    freq = ROPE_THETA ** (-jnp.arange(n_freq, dtype=F32) / n_freq)
    ang = jnp.stack([r[:, None] * freq, col[:, None] * freq], axis=1)
    dd = np.arange(ROPE_DIM)
    axis, half, fr = dd // (ROPE_DIM // 2), (dd % (ROPE_DIM // 2)) // n_freq, dd % n_freq
    c32 = jnp.cos(ang)[:, axis, fr]
    s32 = jnp.sin(ang)[:, axis, fr] * jnp.where(half == 0, -1.0, 1.0).astype(F32)
    pad = jnp.zeros((n_tok, HEAD_PAD - NOPE_DIM - ROPE_DIM), F32)
    ones = jnp.ones((n_tok, NOPE_DIM), F32)
    ck = jnp.concatenate([ones, c32, pad], axis=1)
    sk = jnp.concatenate([0.0 * ones, s32, pad], axis=1)
    q_scale = ATTN_SCALE * LOG2_E
    return {"cq": ck * q_scale, "sq": sk * q_scale, "ck": ck, "sk": sk}


def _prep_weights(w_in, q_norm_g, w_uq, kv_norm_g, w_ukv, w_o_mla, conv_w, conv_b, conv_ln_g,
                  conv_ln_b, w_pw, w_out):
    d = w_in.shape[0]
    q_rank = q_norm_g.shape[0]
    kv_rank = kv_norm_g.shape[0]
    off_kv = q_rank
    off_kr = q_rank + kv_rank
    off_conv = off_kr + ROPE_DIM
    off_gate = off_conv + 2 * d
    qk = NOPE_DIM + ROPE_DIM
    tail = HEAD_PAD - qk

    def lane_pad(w, before, after):
        return jnp.pad(w, [(0, 0)] * (w.ndim - 1) + [(before, after)])

    def partner(w):
        g = w.reshape(w.shape[:-1] + (2, 2, ROPE_DIM // 4))
        return g[..., ::-1, :].reshape(w.shape)

    def place_rope(w32):
        return lane_pad(w32, NOPE_DIM, tail)

    wkr32 = w_in[:, off_kr:off_conv]
    uq = w_uq.reshape(q_rank, N_HEADS, qk)
    uq_pad = lane_pad(uq, 0, tail)
    uq_sw = lane_pad(partner(uq[:, :, NOPE_DIM:]), NOPE_DIM, tail)
    ukv = w_ukv.reshape(kv_rank, N_HEADS, NOPE_DIM + V_DIM)
    uk_pad = lane_pad(ukv[:, :, :NOPE_DIM], 0, HEAD_PAD - NOPE_DIM)
    uv = ukv[:, :, NOPE_DIM:]
    even = (np.arange(N_HEADS) % 2 == 0)[None, :, None]
    uv_pad = jnp.concatenate([jnp.where(even, uv, 0.0), jnp.where(even, 0.0, uv)], axis=-1)
    hp = N_HEADS * HEAD_PAD
    row = lambda v: v.reshape(1, -1).astype(F32)
    return {
        "wq": w_in[:, :off_kv].astype(BF16),
        "wkvc": w_in[:, off_kv:off_kr].astype(BF16),
        "wkr": place_rope(wkr32).astype(BF16),
        "wkrs": place_rope(partner(wkr32)).astype(BF16),
        "wca": w_in[:, off_conv:off_conv + d].astype(BF16),
        "wcg": w_in[:, off_conv + d:off_gate].astype(BF16),
        "wgate": w_in[:, off_gate:].astype(BF16),
        "qg": row(q_norm_g),
        "wuq": uq_pad.reshape(q_rank, hp).astype(BF16),
        "wuqs": uq_sw.reshape(q_rank, hp).astype(BF16),
        "kvg": row(kv_norm_g),
        "wukv": jnp.concatenate([uk_pad.reshape(kv_rank, hp), uv_pad.reshape(kv_rank, hp)], axis=1).astype(BF16),
        "womla": w_o_mla.astype(BF16),
        "wpw": w_pw.astype(BF16),
        "wout": w_out.astype(BF16),
        "cw": conv_w.astype(F32),
        "cb": row(conv_b),
        "lng": row(conv_ln_g),
        "lnb": row(conv_ln_b),
    }


def kernel(x, c, ctx, c_ctx, w_mod, b_mod, norm1_g, norm2_g, w_in, q_norm_g, w_uq, kv_norm_g, w_ukv,
           w_o_mla, conv_w, conv_b, conv_ln_g, conv_ln_b, w_pw, w_out, w_pq, sub_keys, u_experts,
           v_experts, final_g):
    assert w_mod.shape[0] == 1, "single-layer configuration"
    b, l, d = x.shape
    assert c.shape[0] + 1 <= 8
    row = lambda v: v.reshape(1, -1).astype(F32)

    cvec = jnp.concatenate([c, c_ctx[None, :], jnp.zeros((SUBLANES - b - 1, d), F32)], axis=0)
    mod = _mod_call(cvec, w_mod[0], b_mod[0].reshape(1, -1))
    mod3 = mod.reshape(8, 1, 6 * d)

    wts = _prep_weights(w_in[0], q_norm_g[0], w_uq[0], kv_norm_g[0], w_ukv[0], w_o_mla[0], conv_w[0],
                        conv_b[0], conv_ln_g[0], conv_ln_b[0], w_pw[0], w_out[0])
    tabs = _rope_tables(l)
    g1 = row(norm1_g[0])

    tm = min(512, l)
    q, kl, vl, y, gate = _inproj_call(x, mod3, g1, wts, tabs, tm)
    kc, vc = _ctxkv_call(ctx, mod3, b, g1, wts)
    attn = _attn_call(q, kl, vl, kc, vc, tq=min(512, l), tk=min(2048, l))
    x1 = _merge_call(x, mod3, attn, y, gate, wts, tm=min(256, l))

    n_keys = sub_keys.shape[3]
    keys = sub_keys[0].reshape(2 * P_HEADS, n_keys, -1).astype(BF16)
    wpqt = w_pq[0].T.astype(BF16)
    u_b = u_experts[0].astype(BF16)
    vt_b = v_experts[0].T.astype(BF16)
    out = _peer_call(x1.reshape(b * l, d), mod3, row(norm2_g[0]), wpqt, keys, u_b, vt_b, row(final_g),
                     tokens_per_batch=l, tt=min(512, l), et=512)
    return out.reshape(b, l, d)
```

```python
import functools

import jax
import jax.numpy as jnp
import numpy as np
from jax import lax
from jax.experimental import pallas as pl
from jax.experimental.pallas import tpu as pltpu

F32 = jnp.float32
BF16 = jnp.bfloat16

EPS = 1e-6
GRID_W = 64
N_HEADS = 16
NOPE_DIM = 64
ROPE_DIM = 32
V_DIM = 64
ROPE_THETA = 10000.0
ATTN_SCALE = (NOPE_DIM + ROPE_DIM) ** -0.5
LOG2_E = 1.4426950408889634
CONV_WIDTH = 31
CONV_HALF = CONV_WIDTH // 2
P_HEADS = 8
P_TOPK = 16

LANES = 128
SUBLANES = 8
BF16_ROWS = 16
HEAD_PAD = LANES
HALO = 16
R_PER_PIECE = 2
VMEM_LIMIT = 56 * 1024 * 1024

_NT = (((1,), (1,)), ((), ()))


def _dot(a, b):
    return jnp.dot(a, b, preferred_element_type=F32)


def _dot_nt(a, b):
    return lax.dot_general(a, b, _NT, preferred_element_type=F32)


def _rms(x):
    return x * lax.rsqrt(jnp.mean(x * x, axis=-1, keepdims=True) + EPS)


def _cparams(sem):
    return pltpu.CompilerParams(dimension_semantics=sem, vmem_limit_bytes=VMEM_LIMIT)


def _mod_kernel(c_ref, w_ref, b_ref, o_ref):
    cv = c_ref[...]
    s = cv * jax.nn.sigmoid(cv)
    o_ref[...] = jnp.dot(s, w_ref[...], preferred_element_type=F32,
                         precision=lax.Precision.HIGHEST) + b_ref[...]


def _mod_call(cvec, w_mod, b_mod):
    rows, d = cvec.shape
    n = w_mod.shape[1]
    tn = d
    return pl.pallas_call(
        _mod_kernel,
        grid=(n // tn,),
        in_specs=[pl.BlockSpec((rows, d), lambda j: (0, 0)),
                  pl.BlockSpec((d, tn), lambda j: (0, j)),
                  pl.BlockSpec((1, tn), lambda j: (0, j))],
        out_specs=pl.BlockSpec((rows, tn), lambda j: (0, j)),
        out_shape=jax.ShapeDtypeStruct((rows, n), F32),
        compiler_params=_cparams(("arbitrary",)),
        name="mod",
    )(cvec, w_mod, b_mod)


def _modulated_norm(x, g, shift, scale):
    return (_rms(x) * g) * (1.0 + scale) + shift


def _kv_from_h(hb, wkvc_ref, kvg_ref, wukv_ref):
    ckv = _dot(hb, wkvc_ref[...])
    ckvn = (_rms(ckv) * kvg_ref[...]).astype(BF16)
    return _dot(ckvn, wukv_ref[...])


def _inproj_kernel(x_ref, mod_ref, g1_ref, wq_ref, wkvc_ref, wkr_ref, wkrs_ref, wca_ref, wcg_ref,
                   wgate_ref, qg_ref, wuq_ref, wuqs_ref, kvg_ref, wukv_ref,
                   cq_ref, sq_ref, ck_ref, sk_ref,
                   q_out, k_out, v_out, y_out, gate_out):
    d = x_ref.shape[-1]
    hp = N_HEADS * HEAD_PAD
    m = mod_ref[0]
    hb = _modulated_norm(x_ref[0], g1_ref[...], m[:, 0:d], m[:, d:2 * d]).astype(BF16)

    cqn = (_rms(_dot(hb, wq_ref[...])) * qg_ref[...]).astype(BF16)
    qa = _dot(cqn, wuq_ref[...])
    qb = _dot(cqn, wuqs_ref[...])
    cq, sq = cq_ref[...], sq_ref[...]
    for h in range(N_HEADS):
        sl = slice(h * HEAD_PAD, (h + 1) * HEAD_PAD)
        q_out[0, :, sl] = (qa[:, sl] * cq + qb[:, sl] * sq).astype(BF16)

    kv = _kv_from_h(hb, wkvc_ref, kvg_ref, wukv_ref)
    kr = _dot(hb, wkr_ref[...]) * ck_ref[...] + _dot(hb, wkrs_ref[...]) * sk_ref[...]
    for h in range(N_HEADS):
        sl = slice(h * HEAD_PAD, (h + 1) * HEAD_PAD)
        k_out[0, :, sl] = (kv[:, sl] + kr).astype(BF16)
    v_out[0] = kv[:, hp:].astype(BF16)

    a = _dot(hb, wca_ref[...])
    g = _dot(hb, wcg_ref[...])
    y_out[0] = (a * jax.nn.sigmoid(g)).astype(BF16)
    gate_out[0] = jax.nn.sigmoid(_dot(hb, wgate_ref[...])).astype(BF16)


def _const_spec(arr):
    nd = arr.ndim
    return pl.BlockSpec(arr.shape, lambda *_: (0,) * nd)


def _inproj_call(x, mod3, g1, wts, tabs, tm):
    b, l, d = x.shape
    hp = N_HEADS * HEAD_PAD
    nt = l // tm
    weights = [wts[k] for k in ("wq", "wkvc", "wkr", "wkrs", "wca", "wcg", "wgate",
                                "qg", "wuq", "wuqs", "kvg", "wukv")]
    tok = lambda w: pl.BlockSpec((1, tm, w), lambda bi, i: (bi, i, 0))
    tab = pl.BlockSpec((tm, HEAD_PAD), lambda bi, i: (i, 0))
    return pl.pallas_call(
        _inproj_kernel,
        grid=(b, nt),
        in_specs=[tok(d),
                  pl.BlockSpec((1, 1, mod3.shape[-1]), lambda bi, i: (bi, 0, 0)),
                  _const_spec(g1)] + [_const_spec(w) for w in weights] + [tab] * 4,
        out_specs=[tok(hp), tok(hp), tok(hp), tok(d), tok(2 * d)],
        out_shape=[jax.ShapeDtypeStruct((b, l, hp), BF16),
                   jax.ShapeDtypeStruct((b, l, hp), BF16),
                   jax.ShapeDtypeStruct((b, l, hp), BF16),
                   jax.ShapeDtypeStruct((b, l, d), BF16),
                   jax.ShapeDtypeStruct((b, l, 2 * d), BF16)],
        compiler_params=_cparams(("parallel", "parallel")),
        name="in_proj",
    )(x, mod3, g1, *weights, tabs["cq"], tabs["sq"], tabs["ck"], tabs["sk"])


def _ctxkv_kernel(x_ref, mod_ref, g1_ref, wkvc_ref, wkr_ref, kvg_ref, wukv_ref, k_out, v_out):
    d = x_ref.shape[-1]
    hp = N_HEADS * HEAD_PAD
    m = mod_ref[0]
    hb = _modulated_norm(x_ref[0], g1_ref[...], m[:, 0:d], m[:, d:2 * d]).astype(BF16)
    kv = _kv_from_h(hb, wkvc_ref, kvg_ref, wukv_ref)
    kr = _dot(hb, wkr_ref[...])
    for h in range(N_HEADS):
        sl = slice(h * HEAD_PAD, (h + 1) * HEAD_PAD)
        k_out[0, :, sl] = (kv[:, sl] + kr).astype(BF16)
    v_out[0] = kv[:, hp:].astype(BF16)


def _ctxkv_call(ctx, mod3, ctx_row, g1, wts):
    b, n, d = ctx.shape
    hp = N_HEADS * HEAD_PAD
    weights = [wts[k] for k in ("wkvc", "wkr", "kvg", "wukv")]
    tok = lambda w: pl.BlockSpec((1, n, w), lambda bi: (bi, 0, 0))
    return pl.pallas_call(
        _ctxkv_kernel,
        grid=(b,),
        in_specs=[tok(d),
                  pl.BlockSpec((1, 1, mod3.shape[-1]), lambda bi: (ctx_row, 0, 0)),
                  _const_spec(g1)] + [_const_spec(w) for w in weights],
        out_specs=[tok(hp), tok(hp)],
        out_shape=[jax.ShapeDtypeStruct((b, n, hp), BF16)] * 2,
        compiler_params=_cparams(("parallel",)),
        name="ctx_kv",
    )(ctx, mod3, g1, *weights)


def _attn_kernel(q_ref, kl_ref, vl_ref, kc_ref, vc_ref, o_ref, m_ref, l_ref, acc_ref, *, tk):
    n_chunks = kl_ref.shape[1] // tk
    m_ref[...] = jnp.full(m_ref.shape, -jnp.inf, F32)
    l_ref[...] = jnp.zeros(l_ref.shape, F32)
    acc_ref[...] = jnp.zeros(acc_ref.shape, F32)

    def update(head, k, v):
        sl = slice(head * HEAD_PAD, (head + 1) * HEAD_PAD)
        s = _dot_nt(q_ref[0, :, sl], k)
        tiles = [s[:, t * LANES:(t + 1) * LANES] for t in range(s.shape[1] // LANES)]
        tile_max = functools.reduce(jnp.maximum, tiles)
        m_old = m_ref[head]
        m_new = jnp.maximum(m_old, jnp.max(tile_max, axis=-1, keepdims=True))
        alpha = jnp.exp2(m_old - m_new)
        ps = [jnp.exp2(t - m_new) for t in tiles]
        l_ref[head] = alpha * l_ref[head] + functools.reduce(jnp.add, ps)
        p = jnp.concatenate([t.astype(BF16) for t in ps], axis=1)
        acc_ref[head] = alpha * acc_ref[head] + _dot(p, v)
        m_ref[head] = m_new

    def body(c, _):
        off = pl.multiple_of(c * tk, tk)
        for head in range(2):
            sl = slice(head * HEAD_PAD, (head + 1) * HEAD_PAD)
            update(head, kl_ref[0, pl.ds(off, tk), sl], vl_ref[0, pl.ds(off, tk), sl])
        return 0

    lax.fori_loop(0, n_chunks, body, 0)
    out = None
    for head in range(2):
        sl = slice(head * HEAD_PAD, (head + 1) * HEAD_PAD)
        update(head, kc_ref[0, :, sl], vc_ref[0, :, sl])
        o = acc_ref[head] * (1.0 / jnp.sum(l_ref[head], axis=-1, keepdims=True))
        out = o if out is None else out + o
    o_ref[0] = out.astype(BF16)


def _attn_call(q, kl, vl, kc, vc, tq, tk):
    b, l, hp = q.shape
    nc = kc.shape[1]
    pairs = N_HEADS // 2
    pw = 2 * HEAD_PAD
    return pl.pallas_call(
        functools.partial(_attn_kernel, tk=tk),
        grid=(b, pairs, l // tq),
        in_specs=[pl.BlockSpec((1, tq, pw), lambda bi, p, i: (bi, i, p)),
                  pl.BlockSpec((1, l, pw), lambda bi, p, i: (bi, 0, p)),
                  pl.BlockSpec((1, l, pw), lambda bi, p, i: (bi, 0, p)),
                  pl.BlockSpec((1, nc, pw), lambda bi, p, i: (bi, 0, p)),
                  pl.BlockSpec((1, nc, pw), lambda bi, p, i: (bi, 0, p))],
        out_specs=pl.BlockSpec((1, tq, HEAD_PAD), lambda bi, p, i: (bi, i, p)),
        out_shape=jax.ShapeDtypeStruct((b, l, pairs * HEAD_PAD), BF16),
        scratch_shapes=[pltpu.VMEM((2, tq, HEAD_PAD), F32)] * 3,
        compiler_params=_cparams(("parallel", "parallel", "parallel")),
        name="attention",
    )(q, kl, vl, kc, vc)


def _merge_kernel(x_ref, mod_ref, attn_ref, y_ref, yp_ref, yn_ref, gate_ref,
                  womla_ref, wpw_ref, wout_ref, cw_ref, cb_ref, lng_ref, lnb_ref,
                  o_ref, ybuf, sbuf, cbuf, *, rows):
    i = pl.program_id(1)
    last = pl.num_programs(1) - 1
    tm, d = y_ref.shape[1], y_ref.shape[2]

    prev = yp_ref[0].astype(F32)
    nxt = yn_ref[0].astype(F32)
    ybuf[0:HALO, :] = jnp.where(i > 0, prev, 0.0)
    ybuf[HALO:HALO + tm, :] = y_ref[0].astype(F32)
    ybuf[HALO + tm:2 * HALO + tm, :] = jnp.where(i < last, nxt, 0.0)
    span = sbuf.shape[1]
    for p in range(SUBLANES):
        sbuf[p] = ybuf[p:p + span, :]

    cb = cb_ref[...]

    def conv_rows(r, _):
        base = pl.multiple_of(r * rows, rows)
        acc = jnp.broadcast_to(cb, (rows, d))
        for k in range(CONV_WIDTH):
            q, p = divmod(HALO - CONV_HALF + k, SUBLANES)
            acc = acc + cw_ref[k:k + 1, :] * sbuf[p, pl.ds(base + q * SUBLANES, rows), :]
        cbuf[pl.ds(base, rows), :] = acc
        return 0

    lax.fori_loop(0, tm // rows, conv_rows, 0)

    yc = cbuf[...]
    mu = jnp.mean(yc, axis=-1, keepdims=True)
    cen = yc - mu
    var = jnp.mean(cen * cen, axis=-1, keepdims=True)
    ln = cen * lax.rsqrt(var + EPS) * lng_ref[...] + lnb_ref[...]
    act = (ln * jax.nn.sigmoid(ln)).astype(BF16)
    y_c = _dot(act, wpw_ref[...])
    y_a = _dot(attn_ref[0], womla_ref[...])
    gate = gate_ref[0]
    z = gate[:, :d].astype(F32) * y_a + gate[:, d:].astype(F32) * y_c
    zo = _dot(z.astype(BF16), wout_ref[...])
    m = mod_ref[0]
    o_ref[0] = x_ref[0] + m[:, 2 * d:3 * d] * zo


def _merge_call(x, mod3, attn, y, gate, wts, tm):
    b, l, d = x.shape
    nt = l // tm
    hb = tm // HALO
    nhb = l // HALO
    weights = [wts[k] for k in ("womla", "wpw", "wout", "cw", "cb", "lng", "lnb")]
    tok = lambda w: pl.BlockSpec((1, tm, w), lambda bi, i: (bi, i, 0))
    return pl.pallas_call(
        functools.partial(_merge_kernel, rows=32),
        grid=(b, nt),
        in_specs=[tok(d),
                  pl.BlockSpec((1, 1, mod3.shape[-1]), lambda bi, i: (bi, 0, 0)),
                  tok(d), tok(d),
                  pl.BlockSpec((1, HALO, d), lambda bi, i: (bi, jnp.maximum(i * hb - 1, 0), 0)),
                  pl.BlockSpec((1, HALO, d), lambda bi, i: (bi, jnp.minimum((i + 1) * hb, nhb - 1), 0)),
                  tok(2 * d)] + [_const_spec(w) for w in weights],
        out_specs=tok(d),
        out_shape=jax.ShapeDtypeStruct((b, l, d), F32),
        scratch_shapes=[pltpu.VMEM((tm + 2 * HALO, d), F32),
                        pltpu.VMEM((SUBLANES, tm + 2 * HALO - SUBLANES, d), F32),
                        pltpu.VMEM((tm, d), F32)],
        compiler_params=_cparams(("parallel", "parallel")),
        name="merge",
    )(x, mod3, attn, y, y, y, gate, *weights)


def _topk_rank(s, exact):
    nk = s.shape[0]
    key_id = lax.broadcasted_iota(jnp.int32, s.shape, 0)
    rank = jnp.full(s.shape, float(P_TOPK), F32)
    top_id = lax.broadcasted_iota(jnp.int32, (P_TOPK, s.shape[1]), 0)
    tops = jnp.zeros((P_TOPK, s.shape[1]), F32)
    for r in range(P_TOPK):
        cur = jnp.max(s, axis=0, keepdims=True)
        hit = s == cur
        if exact:
            first = jnp.min(jnp.where(hit, key_id, nk), axis=0, keepdims=True)
            hit = key_id == first
        rank = jnp.where(hit, float(r), rank)
        s = jnp.where(hit, -jnp.inf, s)
        tops = jnp.where(top_id == r, cur, tops)
    return tops, rank


def _rank_is_clean(rank):
    kept = jnp.sum(jnp.where(rank < float(P_TOPK), 1.0, 0.0), axis=0, keepdims=True)
    return jnp.max(jnp.abs(kept - float(P_TOPK))) == 0.0


def _staircase(a, b):
    half = P_TOPK // 2
    a_lo, a_hi = a[:half], a[half:]
    row_lo = lax.broadcasted_iota(jnp.int32, a_lo.shape, 0)
    row_hi = row_lo + half
    cnt_lo = jnp.zeros(a_lo.shape, F32)
    cnt_hi = jnp.zeros(a_hi.shape, F32)
    f_lo = a_lo + b[0:1]
    f_hi = a_hi + b[0:1]
    for _ in range(P_TOPK):
        best = jnp.max(jnp.maximum(f_lo, f_hi), axis=0, keepdims=True)
        cand = jnp.minimum(jnp.where(f_lo == best, row_lo, P_TOPK), jnp.where(f_hi == best, row_hi, P_TOPK))
        pick = jnp.min(cand, axis=0, keepdims=True)
        hit_lo = row_lo == pick
        hit_hi = row_hi == pick
        cnt_lo = jnp.where(hit_lo, cnt_lo + 1.0, cnt_lo)
        cnt_hi = jnp.where(hit_hi, cnt_hi + 1.0, cnt_hi)
        nxt = jnp.full(a_lo.shape, -jnp.inf, F32)
        for k in range(1, P_TOPK):
            nxt = jnp.where(cnt_lo == float(k), b[k:k + 1], nxt)
        f_lo = jnp.where(hit_lo, a_lo + nxt, f_lo)
        f_hi = jnp.where(hit_hi, -jnp.inf, f_hi)
    return jnp.concatenate([cnt_lo, cnt_hi], axis=0)


def _peer_kernel(x_ref, mod_ref, g2_ref, wpqt_ref, keys_ref, u_ref, vt_ref, fg_ref, o_ref,
                 h2_ref, sa_ref, rc_ref, top_ref, b2b_ref, rk2b_ref, at0_ref, at1_ref, gt0_ref, gt1_ref,
                 acc_ref):
    s = pl.program_id(1)
    n_steps = pl.num_programs(1)
    tt, d = x_ref.shape
    nlt = tt // LANES
    nk = keys_ref.shape[1]
    et = u_ref.shape[0] // 2
    ke = et // nk
    m = mod_ref[0]

    @pl.when(s == 0)
    def _prepare():
        h2 = _modulated_norm(x_ref[...], g2_ref[...], m[:, 3 * d:4 * d], m[:, 4 * d:5 * d])
        h2_ref[...] = h2.T.astype(BF16)
        dk = keys_ref.shape[2]
        qt = _dot(wpqt_ref[...], h2_ref[...]).astype(BF16)
        for hp in range(2 * P_HEADS):
            sc = _dot(keys_ref[hp], qt[hp * dk:(hp + 1) * dk, :])
            for lt in range(nlt):
                sa_ref[hp, lt] = sc[:, lt * LANES:(lt + 1) * LANES]

        n_rank = 2 * P_HEADS * nlt

        def rank_units(u, _):
            units = []
            for v in (u, u + n_rank // 2):
                hp, lt = v // nlt, v % nlt
                tops, rank = _topk_rank(sa_ref[hp, lt], exact=False)
                top_ref[hp, lt] = tops
                rc_ref[hp, lt] = rank
                units.append((hp, lt, _rank_is_clean(rank)))
            for hp, lt, clean in units:
                @pl.when(jnp.logical_not(clean))
                def _redo(hp=hp, lt=lt):
                    tops, rank = _topk_rank(sa_ref[hp, lt], exact=True)
                    top_ref[hp, lt] = tops
                    rc_ref[hp, lt] = rank
            return 0

        lax.fori_loop(0, n_rank // 2, rank_units, 0)

        n_gate = P_HEADS * nlt

        def gate_units(u, _):
            for v in (u, u + n_gate // 2):
                h, lt = v // nlt, v % nlt
                a, b = top_ref[2 * h, lt], top_ref[2 * h + 1, lt]
                cnt = _staircase(a, b)
                ea = jnp.exp(a - a[0:1])
                eb = jnp.exp(b - b[0:1])
                cum = jnp.zeros(a.shape, F32)
                for k in range(P_TOPK):
                    cum = cum + jnp.where(cnt > float(k), eb[k:k + 1], 0.0)
                inv_z = 1.0 / jnp.sum(ea * cum, axis=0, keepdims=True)
                r1 = rc_ref[2 * h, lt]
                cnt1 = jnp.zeros(r1.shape, F32)
                for k in range(P_TOPK):
                    cnt1 = jnp.where(r1 == float(k), cnt[k:k + 1], cnt1)
                rc_ref[2 * h, lt] = cnt1
                sa_ref[2 * h, lt] = jnp.exp(sa_ref[2 * h, lt] - a[0:1]) * inv_z
                b2b_ref[h, lt] = jnp.exp(sa_ref[2 * h + 1, lt] - b[0:1]).astype(BF16)
                rk2b_ref[h, lt] = rc_ref[2 * h + 1, lt].astype(BF16)
            return 0

        lax.fori_loop(0, n_gate // 2, gate_units, 0)
        acc_ref[...] = jnp.zeros(acc_ref.shape, F32)
        for ref in (at0_ref, at1_ref, gt0_ref, gt1_ref):
            ref[...] = jnp.zeros(ref.shape, ref.dtype)

    def half_step(half, tile_b, at_src, gt_dst, at_dst, gt_src):
        e1_base = jnp.clip(tile_b * ke, 0, nk - ke)
        groups = nk // BF16_ROWS
        n_trips = ke // R_PER_PIECE
        a_rows = et // n_trips
        c_rows = d // n_trips

        def trip(i, _):
            cr = pl.ds(pl.multiple_of(i * c_rows, c_rows), c_rows)
            acc_ref[cr, :] += _dot(vt_ref[cr, half * et:(half + 1) * et], gt_src[...])
            a_lo = pl.multiple_of(i * a_rows, a_rows)
            u_rows = pl.ds(pl.multiple_of(half * et + a_lo, a_rows), a_rows)
            at_dst[pl.ds(a_lo, a_rows), :] = _dot(u_ref[u_rows, :], h2_ref[...])

            for lt in range(nlt):
                ls = slice(lt * LANES, (lt + 1) * LANES)
                ws = [[jnp.zeros((BF16_ROWS, LANES), BF16) for _ in range(groups)] for _ in range(R_PER_PIECE)]
                for h in range(P_HEADS):
                    b2 = b2b_ref[h, lt]
                    rk2 = rk2b_ref[h, lt]
                    for rr in range(R_PER_PIECE):
                        e1 = e1_base + i * R_PER_PIECE + rr
                        cnt_row = rc_ref[2 * h, lt, pl.ds(e1, 1), :]
                        a_row = sa_ref[2 * h, lt, pl.ds(e1, 1), :]
                        cnt_t = jnp.broadcast_to(cnt_row, (BF16_ROWS, LANES)).astype(BF16)
                        a_t = jnp.broadcast_to(a_row, (BF16_ROWS, LANES)).astype(BF16)
                        for g in range(groups):
                            gs = slice(g * BF16_ROWS, (g + 1) * BF16_ROWS)
                            ws[rr][g] = ws[rr][g] + a_t * jnp.where(rk2[gs] < cnt_t, b2[gs], jnp.zeros_like(a_t))
                for rr in range(R_PER_PIECE):
                    er = pl.ds(pl.multiple_of((i * R_PER_PIECE + rr) * nk, nk), nk)
                    pre = at_src[er, ls]
                    act = 0.5 * pre * (1.0 + lax.erf(pre * np.float32(2.0 ** -0.5)))
                    gt_dst[er, ls] = act.astype(BF16) * jnp.concatenate(ws[rr], axis=0)
            return 0

        lax.fori_loop(0, n_trips, trip, 0)

    half_step(0, 2 * s - 1, at1_ref, gt1_ref, at0_ref, gt0_ref)
    half_step(1, 2 * s, at0_ref, gt0_ref, at1_ref, gt1_ref)

    @pl.when(s == n_steps - 1)
    def _finish():
        x2 = x_ref[...] + m[:, 5 * d:6 * d] * acc_ref[...].T
        o_ref[...] = _rms(x2) * fg_ref[...]


def _peer_call(x1, mod3, g2, wpqt, keys, u_b, vt_b, fg, tokens_per_batch, tt, et):
    n, d = x1.shape
    ne = u_b.shape[0]
    nk = keys.shape[1]
    nlt = tt // LANES
    n_pairs = ne // (2 * et)
    tiles_per_batch = tokens_per_batch // tt
    unit = (2 * P_HEADS, nlt, nk, LANES)
    return pl.pallas_call(
        _peer_kernel,
        grid=(n // tt, n_pairs + 1),
        in_specs=[pl.BlockSpec((tt, d), lambda t, j: (t, 0)),
                  pl.BlockSpec((1, 1, mod3.shape[-1]), lambda t, j: (t // tiles_per_batch, 0, 0)),
                  _const_spec(g2), _const_spec(wpqt), _const_spec(keys),
                  pl.BlockSpec((2 * et, d), lambda t, j: (jnp.minimum(j, n_pairs - 1), 0)),
                  pl.BlockSpec((d, 2 * et), lambda t, j: (0, jnp.maximum(j - 1, 0))),
                  _const_spec(fg)],
        out_specs=pl.BlockSpec((tt, d), lambda t, j: (t, 0)),
        out_shape=jax.ShapeDtypeStruct((n, d), F32),
        scratch_shapes=[pltpu.VMEM((d, tt), BF16),
                        pltpu.VMEM(unit, F32),
                        pltpu.VMEM(unit, F32),
                        pltpu.VMEM((2 * P_HEADS, nlt, P_TOPK, LANES), F32),
                        pltpu.VMEM((P_HEADS, nlt, nk, LANES), BF16),
                        pltpu.VMEM((P_HEADS, nlt, nk, LANES), BF16),
                        pltpu.VMEM((et, tt), F32), pltpu.VMEM((et, tt), F32),
                        pltpu.VMEM((et, tt), BF16), pltpu.VMEM((et, tt), BF16),
                        pltpu.VMEM((d, tt), F32)],
        compiler_params=_cparams(("parallel", "arbitrary")),
        name="peer",
    )(x1, mod3, g2, wpqt, keys, u_b, vt_b, fg)


def _rope_tables(n_tok):
    rows = n_tok // GRID_W
    r = jnp.repeat(jnp.arange(rows, dtype=jnp.int32), GRID_W)
    col = jnp.tile(jnp.arange(GRID_W, dtype=jnp.int32), rows)
    n_freq = ROPE_DIM // 4
    freq = ROPE_THETA ** (-jnp.arange(n_freq, dtype=F32) / n_freq)
    ang = jnp.stack([r[:, None] * freq, col[:, None] * freq], axis=1)
    dd = np.arange(ROPE_DIM)
    axis, half, fr = dd // (ROPE_DIM // 2), (dd % (ROPE_DIM // 2)) // n_freq, dd % n_freq
    c32 = jnp.cos(ang)[:, axis, fr]
    s32 = jnp.sin(ang)[:, axis, fr] * jnp.where(half == 0, -1.0, 1.0).astype(F32)
    pad = jnp.zeros((n_tok, HEAD_PAD - NOPE_DIM - ROPE_DIM), F32)
    ones = jnp.ones((n_tok, NOPE_DIM), F32)
    ck = jnp.concatenate([ones, c32, pad], axis=1)
    sk = jnp.concatenate([0.0 * ones, s32, pad], axis=1)
    q_scale = ATTN_SCALE * LOG2_E
    return {"cq": ck * q_scale, "sq": sk * q_scale, "ck": ck, "sk": sk}


def _prep_weights(w_in, q_norm_g, w_uq, kv_norm_g, w_ukv, w_o_mla, conv_w, conv_b, conv_ln_g,
                  conv_ln_b, w_pw, w_out):
    d = w_in.shape[0]
    q_rank = q_norm_g.shape[0]
    kv_rank = kv_norm_g.shape[0]
    off_kv = q_rank
    off_kr = q_rank + kv_rank
    off_conv = off_kr + ROPE_DIM
    off_gate = off_conv + 2 * d
    qk = NOPE_DIM + ROPE_DIM
    tail = HEAD_PAD - qk

    def lane_pad(w, before, after):
        return jnp.pad(w, [(0, 0)] * (w.ndim - 1) + [(before, after)])

    def partner(w):
        g = w.reshape(w.shape[:-1] + (2, 2, ROPE_DIM // 4))
        return g[..., ::-1, :].reshape(w.shape)

    def place_rope(w32):
        return lane_pad(w32, NOPE_DIM, tail)

    wkr32 = w_in[:, off_kr:off_conv]
    uq = w_uq.reshape(q_rank, N_HEADS, qk)
    uq_pad = lane_pad(uq, 0, tail)
    uq_sw = lane_pad(partner(uq[:, :, NOPE_DIM:]), NOPE_DIM, tail)
    ukv = w_ukv.reshape(kv_rank, N_HEADS, NOPE_DIM + V_DIM)
    uk_pad = lane_pad(ukv[:, :, :NOPE_DIM], 0, HEAD_PAD - NOPE_DIM)
    uv = ukv[:, :, NOPE_DIM:]
    even = (np.arange(N_HEADS) % 2 == 0)[None, :, None]
    uv_pad = jnp.concatenate([jnp.where(even, uv, 0.0), jnp.where(even, 0.0, uv)], axis=-1)
    hp = N_HEADS * HEAD_PAD
    row = lambda v: v.reshape(1, -1).astype(F32)
    return {
        "wq": w_in[:, :off_kv].astype(BF16),
        "wkvc": w_in[:, off_kv:off_kr].astype(BF16),
        "wkr": place_rope(wkr32).astype(BF16),
        "wkrs": place_rope(partner(wkr32)).astype(BF16),
        "wca": w_in[:, off_conv:off_conv + d].astype(BF16),
        "wcg": w_in[:, off_conv + d:off_gate].astype(BF16),
        "wgate": w_in[:, off_gate:].astype(BF16),
        "qg": row(q_norm_g),
        "wuq": uq_pad.reshape(q_rank, hp).astype(BF16),
        "wuqs": uq_sw.reshape(q_rank, hp).astype(BF16),
        "kvg": row(kv_norm_g),
        "wukv": jnp.concatenate([uk_pad.reshape(kv_rank, hp), uv_pad.reshape(kv_rank, hp)], axis=1).astype(BF16),
        "womla": w_o_mla.astype(BF16),
        "wpw": w_pw.astype(BF16),
        "wout": w_out.astype(BF16),
        "cw": conv_w.astype(F32),
        "cb": row(conv_b),
        "lng": row(conv_ln_g),
        "lnb": row(conv_ln_b),
    }


def kernel(x, c, ctx, c_ctx, w_mod, b_mod, norm1_g, norm2_g, w_in, q_norm_g, w_uq, kv_norm_g, w_ukv,
           w_o_mla, conv_w, conv_b, conv_ln_g, conv_ln_b, w_pw, w_out, w_pq, sub_keys, u_experts,
           v_experts, final_g):
    assert w_mod.shape[0] == 1, "single-layer configuration"
    b, l, d = x.shape
    assert c.shape[0] + 1 <= 8
    row = lambda v: v.reshape(1, -1).astype(F32)

    cvec = jnp.concatenate([c, c_ctx[None, :], jnp.zeros((SUBLANES - b - 1, d), F32)], axis=0)
    mod = _mod_call(cvec, w_mod[0], b_mod[0].reshape(1, -1))
    mod3 = mod.reshape(8, 1, 6 * d)

    wts = _prep_weights(w_in[0], q_norm_g[0], w_uq[0], kv_norm_g[0], w_ukv[0], w_o_mla[0], conv_w[0],
                        conv_b[0], conv_ln_g[0], conv_ln_b[0], w_pw[0], w_out[0])
    tabs = _rope_tables(l)
    g1 = row(norm1_g[0])

    tm = min(512, l)
    q, kl, vl, y, gate = _inproj_call(x, mod3, g1, wts, tabs, tm)
    kc, vc = _ctxkv_call(ctx, mod3, b, g1, wts)
    attn = _attn_call(q, kl, vl, kc, vc, tq=min(1024, l), tk=min(1024, l))
    x1 = _merge_call(x, mod3, attn, y, gate, wts, tm=min(256, l))

    n_keys = sub_keys.shape[3]
    keys = sub_keys[0].reshape(2 * P_HEADS, n_keys, -1).astype(BF16)
    wpqt = w_pq[0].T.astype(BF16)
    u_b = u_experts[0].astype(BF16)
    vt_b = v_experts[0].T.astype(BF16)
    out = _peer_call(x1.reshape(b * l, d), mod3, row(norm2_g[0]), wpqt, keys, u_b, vt_b, row(final_g),
                     tokens_per_batch=l, tt=min(512, l), et=512)
    return out.reshape(b, l, d)
```

```python
import functools

import jax
import jax.numpy as jnp
import numpy as np
from jax import lax
from jax.experimental import pallas as pl
from jax.experimental.pallas import tpu as pltpu

F32 = jnp.float32
BF16 = jnp.bfloat16

EPS = 1e-6
GRID_W = 64
N_HEADS = 16
NOPE_DIM = 64
ROPE_DIM = 32
V_DIM = 64
ROPE_THETA = 10000.0
ATTN_SCALE = (NOPE_DIM + ROPE_DIM) ** -0.5
LOG2_E = 1.4426950408889634
CONV_WIDTH = 31
CONV_HALF = CONV_WIDTH // 2
P_HEADS = 8
P_TOPK = 16

LANES = 128
SUBLANES = 8
BF16_ROWS = 16
HEAD_PAD = LANES
HALO = 16
R_PER_PIECE = 2
VMEM_LIMIT = 56 * 1024 * 1024

_NT = (((1,), (1,)), ((), ()))


def _dot(a, b):
    return jnp.dot(a, b, preferred_element_type=F32)


def _dot_nt(a, b):
    return lax.dot_general(a, b, _NT, preferred_element_type=F32)


def _rms(x):
    return x * lax.rsqrt(jnp.mean(x * x, axis=-1, keepdims=True) + EPS)


def _cparams(sem):
    return pltpu.CompilerParams(dimension_semantics=sem, vmem_limit_bytes=VMEM_LIMIT)


def _mod_kernel(c_ref, w_ref, b_ref, o_ref):
    cv = c_ref[...]
    s = cv * jax.nn.sigmoid(cv)
    o_ref[...] = jnp.dot(s, w_ref[...], preferred_element_type=F32,
                         precision=lax.Precision.HIGHEST) + b_ref[...]


def _mod_call(cvec, w_mod, b_mod):
    rows, d = cvec.shape
    n = w_mod.shape[1]
    tn = d
    return pl.pallas_call(
        _mod_kernel,
        grid=(n // tn,),
        in_specs=[pl.BlockSpec((rows, d), lambda j: (0, 0)),
                  pl.BlockSpec((d, tn), lambda j: (0, j)),
                  pl.BlockSpec((1, tn), lambda j: (0, j))],
        out_specs=pl.BlockSpec((rows, tn), lambda j: (0, j)),
        out_shape=jax.ShapeDtypeStruct((rows, n), F32),
        compiler_params=_cparams(("arbitrary",)),
        name="mod",
    )(cvec, w_mod, b_mod)


def _modulated_norm(x, g, shift, scale):
    return (_rms(x) * g) * (1.0 + scale) + shift


def _kv_from_h(hb, wkvc_ref, kvg_ref, wukv_ref):
    ckv = _dot(hb, wkvc_ref[...])
    ckvn = (_rms(ckv) * kvg_ref[...]).astype(BF16)
    return _dot(ckvn, wukv_ref[...])


def _inproj_kernel(x_ref, mod_ref, g1_ref, wq_ref, wkvc_ref, wkr_ref, wkrs_ref, wca_ref, wcg_ref,
                   wgate_ref, qg_ref, wuq_ref, wuqs_ref, kvg_ref, wukv_ref,
                   cq_ref, sq_ref, ck_ref, sk_ref,
                   q_out, k_out, v_out, y_out, gate_out):
    d = x_ref.shape[-1]
    hp = N_HEADS * HEAD_PAD
    m = mod_ref[0]
    hb = _modulated_norm(x_ref[0], g1_ref[...], m[:, 0:d], m[:, d:2 * d]).astype(BF16)

    cqn = (_rms(_dot(hb, wq_ref[...])) * qg_ref[...]).astype(BF16)
    qa = _dot(cqn, wuq_ref[...])
    qb = _dot(cqn, wuqs_ref[...])
    cq, sq = cq_ref[...], sq_ref[...]
    for h in range(N_HEADS):
        sl = slice(h * HEAD_PAD, (h + 1) * HEAD_PAD)
        q_out[0, :, sl] = (qa[:, sl] * cq + qb[:, sl] * sq).astype(BF16)

    kv = _kv_from_h(hb, wkvc_ref, kvg_ref, wukv_ref)
    kr = _dot(hb, wkr_ref[...]) * ck_ref[...] + _dot(hb, wkrs_ref[...]) * sk_ref[...]
    for h in range(N_HEADS):
        sl = slice(h * HEAD_PAD, (h + 1) * HEAD_PAD)
        k_out[0, :, sl] = (kv[:, sl] + kr).astype(BF16)
    v_out[0] = kv[:, hp:].astype(BF16)

    a = _dot(hb, wca_ref[...])
    g = _dot(hb, wcg_ref[...])
    y_out[0] = (a * jax.nn.sigmoid(g)).astype(BF16)
    gate_out[0] = jax.nn.sigmoid(_dot(hb, wgate_ref[...])).astype(BF16)


def _const_spec(arr):
    nd = arr.ndim
    return pl.BlockSpec(arr.shape, lambda *_: (0,) * nd)


def _inproj_call(x, mod3, g1, wts, tabs, tm):
    b, l, d = x.shape
    hp = N_HEADS * HEAD_PAD
    nt = l // tm
    weights = [wts[k] for k in ("wq", "wkvc", "wkr", "wkrs", "wca", "wcg", "wgate",
                                "qg", "wuq", "wuqs", "kvg", "wukv")]
    tok = lambda w: pl.BlockSpec((1, tm, w), lambda bi, i: (bi, i, 0))
    tab = pl.BlockSpec((tm, HEAD_PAD), lambda bi, i: (i, 0))
    return pl.pallas_call(
        _inproj_kernel,
        grid=(b, nt),
        in_specs=[tok(d),
                  pl.BlockSpec((1, 1, mod3.shape[-1]), lambda bi, i: (bi, 0, 0)),
                  _const_spec(g1)] + [_const_spec(w) for w in weights] + [tab] * 4,
        out_specs=[tok(hp), tok(hp), tok(hp), tok(d), tok(2 * d)],
        out_shape=[jax.ShapeDtypeStruct((b, l, hp), BF16),
                   jax.ShapeDtypeStruct((b, l, hp), BF16),
                   jax.ShapeDtypeStruct((b, l, hp), BF16),
                   jax.ShapeDtypeStruct((b, l, d), BF16),
                   jax.ShapeDtypeStruct((b, l, 2 * d), BF16)],
        compiler_params=_cparams(("parallel", "parallel")),
        name="in_proj",
    )(x, mod3, g1, *weights, tabs["cq"], tabs["sq"], tabs["ck"], tabs["sk"])


def _ctxkv_kernel(x_ref, mod_ref, g1_ref, wkvc_ref, wkr_ref, kvg_ref, wukv_ref, k_out, v_out):
    d = x_ref.shape[-1]
    hp = N_HEADS * HEAD_PAD
    m = mod_ref[0]
    hb = _modulated_norm(x_ref[0], g1_ref[...], m[:, 0:d], m[:, d:2 * d]).astype(BF16)
    kv = _kv_from_h(hb, wkvc_ref, kvg_ref, wukv_ref)
    kr = _dot(hb, wkr_ref[...])
    for h in range(N_HEADS):
        sl = slice(h * HEAD_PAD, (h + 1) * HEAD_PAD)
        k_out[0, :, sl] = (kv[:, sl] + kr).astype(BF16)
    v_out[0] = kv[:, hp:].astype(BF16)


def _ctxkv_call(ctx, mod3, ctx_row, g1, wts):
    b, n, d = ctx.shape
    hp = N_HEADS * HEAD_PAD
    weights = [wts[k] for k in ("wkvc", "wkr", "kvg", "wukv")]
    tok = lambda w: pl.BlockSpec((1, n, w), lambda bi: (bi, 0, 0))
    return pl.pallas_call(
        _ctxkv_kernel,
        grid=(b,),
        in_specs=[tok(d),
                  pl.BlockSpec((1, 1, mod3.shape[-1]), lambda bi: (ctx_row, 0, 0)),
                  _const_spec(g1)] + [_const_spec(w) for w in weights],
        out_specs=[tok(hp), tok(hp)],
        out_shape=[jax.ShapeDtypeStruct((b, n, hp), BF16)] * 2,
        compiler_params=_cparams(("parallel",)),
        name="ctx_kv",
    )(ctx, mod3, g1, *weights)


def _attn_kernel(q_ref, kl_ref, vl_ref, kc_ref, vc_ref, o_ref, m_ref, l_ref, acc_ref, *, tk):
    n_chunks = kl_ref.shape[1] // tk
    m_ref[...] = jnp.full(m_ref.shape, -jnp.inf, F32)
    l_ref[...] = jnp.zeros(l_ref.shape, F32)
    acc_ref[...] = jnp.zeros(acc_ref.shape, F32)

    def update(head, k, v):
        sl = slice(head * HEAD_PAD, (head + 1) * HEAD_PAD)
        s = _dot_nt(q_ref[0, :, sl], k)
        tiles = [s[:, t * LANES:(t + 1) * LANES] for t in range(s.shape[1] // LANES)]
        tile_max = functools.reduce(jnp.maximum, tiles)
        m_old = m_ref[head]
        m_new = jnp.maximum(m_old, jnp.max(tile_max, axis=-1, keepdims=True))
        alpha = jnp.exp2(m_old - m_new)
        ps = [jnp.exp2(t - m_new) for t in tiles]
        l_ref[head] = alpha * l_ref[head] + functools.reduce(jnp.add, ps)
        p = jnp.concatenate([t.astype(BF16) for t in ps], axis=1)
        acc_ref[head] = alpha * acc_ref[head] + _dot(p, v)
        m_ref[head] = m_new

    def body(c, _):
        off = pl.multiple_of(c * tk, tk)
        for head in range(2):
            sl = slice(head * HEAD_PAD, (head + 1) * HEAD_PAD)
            update(head, kl_ref[0, pl.ds(off, tk), sl], vl_ref[0, pl.ds(off, tk), sl])
        return 0

    lax.fori_loop(0, n_chunks, body, 0)
    out = None
    for head in range(2):
        sl = slice(head * HEAD_PAD, (head + 1) * HEAD_PAD)
        update(head, kc_ref[0, :, sl], vc_ref[0, :, sl])
        o = acc_ref[head] * (1.0 / jnp.sum(l_ref[head], axis=-1, keepdims=True))
        out = o if out is None else out + o
    o_ref[0] = out.astype(BF16)


def _attn_call(q, kl, vl, kc, vc, tq, tk):
    b, l, hp = q.shape
    nc = kc.shape[1]
    pairs = N_HEADS // 2
    pw = 2 * HEAD_PAD
    return pl.pallas_call(
        functools.partial(_attn_kernel, tk=tk),
        grid=(b, pairs, l // tq),
        in_specs=[pl.BlockSpec((1, tq, pw), lambda bi, p, i: (bi, i, p)),
                  pl.BlockSpec((1, l, pw), lambda bi, p, i: (bi, 0, p)),
                  pl.BlockSpec((1, l, pw), lambda bi, p, i: (bi, 0, p)),
                  pl.BlockSpec((1, nc, pw), lambda bi, p, i: (bi, 0, p)),
                  pl.BlockSpec((1, nc, pw), lambda bi, p, i: (bi, 0, p))],
        out_specs=pl.BlockSpec((1, tq, HEAD_PAD), lambda bi, p, i: (bi, i, p)),
        out_shape=jax.ShapeDtypeStruct((b, l, pairs * HEAD_PAD), BF16),
        scratch_shapes=[pltpu.VMEM((2, tq, HEAD_PAD), F32)] * 3,
        compiler_params=_cparams(("parallel", "parallel", "parallel")),
        name="attention",
    )(q, kl, vl, kc, vc)


def _merge_kernel(x_ref, mod_ref, attn_ref, y_ref, yp_ref, yn_ref, gate_ref,
                  womla_ref, wpw_ref, wout_ref, cw_ref, cb_ref, lng_ref, lnb_ref,
                  o_ref, ybuf, sbuf, cbuf, *, rows):
    i = pl.program_id(1)
    last = pl.num_programs(1) - 1
    tm, d = y_ref.shape[1], y_ref.shape[2]

    prev = yp_ref[0].astype(F32)
    nxt = yn_ref[0].astype(F32)
    ybuf[0:HALO, :] = jnp.where(i > 0, prev, 0.0)
    ybuf[HALO:HALO + tm, :] = y_ref[0].astype(F32)
    ybuf[HALO + tm:2 * HALO + tm, :] = jnp.where(i < last, nxt, 0.0)
    span = sbuf.shape[1]
    for p in range(SUBLANES):
        sbuf[p] = ybuf[p:p + span, :]

    cb = cb_ref[...]

    def conv_rows(r, _):
        base = pl.multiple_of(r * rows, rows)
        acc = jnp.broadcast_to(cb, (rows, d))
        for k in range(CONV_WIDTH):
            q, p = divmod(HALO - CONV_HALF + k, SUBLANES)
            acc = acc + cw_ref[k:k + 1, :] * sbuf[p, pl.ds(base + q * SUBLANES, rows), :]
        cbuf[pl.ds(base, rows), :] = acc
        return 0

    lax.fori_loop(0, tm // rows, conv_rows, 0)

    yc = cbuf[...]
    mu = jnp.mean(yc, axis=-1, keepdims=True)
    cen = yc - mu
    var = jnp.mean(cen * cen, axis=-1, keepdims=True)
    ln = cen * lax.rsqrt(var + EPS) * lng_ref[...] + lnb_ref[...]
    act = (ln * jax.nn.sigmoid(ln)).astype(BF16)
    y_c = _dot(act, wpw_ref[...])
    y_a = _dot(attn_ref[0], womla_ref[...])
    gate = gate_ref[0]
    z = gate[:, :d].astype(F32) * y_a + gate[:, d:].astype(F32) * y_c
    zo = _dot(z.astype(BF16), wout_ref[...])
    m = mod_ref[0]
    o_ref[0] = x_ref[0] + m[:, 2 * d:3 * d] * zo


def _merge_call(x, mod3, attn, y, gate, wts, tm):
    b, l, d = x.shape
    nt = l // tm
    hb = tm // HALO
    nhb = l // HALO
    weights = [wts[k] for k in ("womla", "wpw", "wout", "cw", "cb", "lng", "lnb")]
    tok = lambda w: pl.BlockSpec((1, tm, w), lambda bi, i: (bi, i, 0))
    return pl.pallas_call(
        functools.partial(_merge_kernel, rows=32),
        grid=(b, nt),
        in_specs=[tok(d),
                  pl.BlockSpec((1, 1, mod3.shape[-1]), lambda bi, i: (bi, 0, 0)),
                  tok(d), tok(d),
                  pl.BlockSpec((1, HALO, d), lambda bi, i: (bi, jnp.maximum(i * hb - 1, 0), 0)),
                  pl.BlockSpec((1, HALO, d), lambda bi, i: (bi, jnp.minimum((i + 1) * hb, nhb - 1), 0)),
                  tok(2 * d)] + [_const_spec(w) for w in weights],
        out_specs=tok(d),
        out_shape=jax.ShapeDtypeStruct((b, l, d), F32),
        scratch_shapes=[pltpu.VMEM((tm + 2 * HALO, d), F32),
                        pltpu.VMEM((SUBLANES, tm + 2 * HALO - SUBLANES, d), F32),
                        pltpu.VMEM((tm, d), F32)],
        compiler_params=_cparams(("parallel", "parallel")),
        name="merge",
    )(x, mod3, attn, y, y, y, gate, *weights)


def _topk_rank(s, exact):
    nk = s.shape[0]
    key_id = lax.broadcasted_iota(jnp.int32, s.shape, 0)
    rank = jnp.full(s.shape, float(P_TOPK), F32)
    top_id = lax.broadcasted_iota(jnp.int32, (P_TOPK, s.shape[1]), 0)
    tops = jnp.zeros((P_TOPK, s.shape[1]), F32)
    for r in range(P_TOPK):
        cur = jnp.max(s, axis=0, keepdims=True)
        hit = s == cur
        if exact:
            first = jnp.min(jnp.where(hit, key_id, nk), axis=0, keepdims=True)
            hit = key_id == first
        rank = jnp.where(hit, float(r), rank)
        s = jnp.where(hit, -jnp.inf, s)
        tops = jnp.where(top_id == r, cur, tops)
    return tops, rank


def _rank_is_clean(rank):
    kept = jnp.sum(jnp.where(rank < float(P_TOPK), 1.0, 0.0), axis=0, keepdims=True)
    return jnp.max(jnp.abs(kept - float(P_TOPK))) == 0.0


def _staircase(a, b):
    half = P_TOPK // 2
    a_lo, a_hi = a[:half], a[half:]
    row_lo = lax.broadcasted_iota(jnp.int32, a_lo.shape, 0)
    row_hi = row_lo + half
    cnt_lo = jnp.zeros(a_lo.shape, F32)
    cnt_hi = jnp.zeros(a_hi.shape, F32)
    f_lo = a_lo + b[0:1]
    f_hi = a_hi + b[0:1]
    for _ in range(P_TOPK):
        best = jnp.max(jnp.maximum(f_lo, f_hi), axis=0, keepdims=True)
        cand = jnp.minimum(jnp.where(f_lo == best, row_lo, P_TOPK), jnp.where(f_hi == best, row_hi, P_TOPK))
        pick = jnp.min(cand, axis=0, keepdims=True)
        hit_lo = row_lo == pick
        hit_hi = row_hi == pick
        cnt_lo = jnp.where(hit_lo, cnt_lo + 1.0, cnt_lo)
        cnt_hi = jnp.where(hit_hi, cnt_hi + 1.0, cnt_hi)
        nxt = jnp.full(a_lo.shape, -jnp.inf, F32)
        for k in range(1, P_TOPK):
            nxt = jnp.where(cnt_lo == float(k), b[k:k + 1], nxt)
        f_lo = jnp.where(hit_lo, a_lo + nxt, f_lo)
        f_hi = jnp.where(hit_hi, -jnp.inf, f_hi)
    return jnp.concatenate([cnt_lo, cnt_hi], axis=0)


def _peer_kernel(x_ref, mod_ref, g2_ref, wpqt_ref, keys_ref, u_ref, vt_ref, fg_ref, o_ref,
                 h2_ref, sa_ref, rc_ref, top_ref, b2b_ref, rk2b_ref, at0_ref, at1_ref, gt0_ref, gt1_ref,
                 acc_ref):
    s = pl.program_id(1)
    n_steps = pl.num_programs(1)
    tt, d = x_ref.shape
    nlt = tt // LANES
    nk = keys_ref.shape[1]
    et = u_ref.shape[0] // 2
    ke = et // nk
    m = mod_ref[0]

    @pl.when(s == 0)
    def _prepare():
        h2 = _modulated_norm(x_ref[...], g2_ref[...], m[:, 3 * d:4 * d], m[:, 4 * d:5 * d])
        h2_ref[...] = h2.T.astype(BF16)
        dk = keys_ref.shape[2]
        qt = _dot(wpqt_ref[...], h2_ref[...]).astype(BF16)
        for hp in range(2 * P_HEADS):
            sc = _dot(keys_ref[hp], qt[hp * dk:(hp + 1) * dk, :])
            for lt in range(nlt):
                sa_ref[hp, lt] = sc[:, lt * LANES:(lt + 1) * LANES]

        n_rank = 2 * P_HEADS * nlt

        def rank_units(u, _):
            units = []
            for v in (u, u + n_rank // 4, u + n_rank // 2, u + 3 * n_rank // 4):
                hp, lt = v // nlt, v % nlt
                tops, rank = _topk_rank(sa_ref[hp, lt], exact=False)
                top_ref[hp, lt] = tops
                rc_ref[hp, lt] = rank
                units.append((hp, lt, _rank_is_clean(rank)))
            for hp, lt, clean in units:
                @pl.when(jnp.logical_not(clean))
                def _redo(hp=hp, lt=lt):
                    tops, rank = _topk_rank(sa_ref[hp, lt], exact=True)
                    top_ref[hp, lt] = tops
                    rc_ref[hp, lt] = rank
            return 0

        lax.fori_loop(0, n_rank // 4, rank_units, 0)

        n_gate = P_HEADS * nlt

        def gate_units(u, _):
            for v in (u, u + n_gate // 2):
                h, lt = v // nlt, v % nlt
                a, b = top_ref[2 * h, lt], top_ref[2 * h + 1, lt]
                cnt = _staircase(a, b)
                ea = jnp.exp(a - a[0:1])
                eb = jnp.exp(b - b[0:1])
                cum = jnp.zeros(a.shape, F32)
                for k in range(P_TOPK):
                    cum = cum + jnp.where(cnt > float(k), eb[k:k + 1], 0.0)
                inv_z = 1.0 / jnp.sum(ea * cum, axis=0, keepdims=True)
                r1 = rc_ref[2 * h, lt]
                cnt1 = jnp.zeros(r1.shape, F32)
                for k in range(P_TOPK):
                    cnt1 = jnp.where(r1 == float(k), cnt[k:k + 1], cnt1)
                rc_ref[2 * h, lt] = cnt1
                sa_ref[2 * h, lt] = jnp.exp(sa_ref[2 * h, lt] - a[0:1]) * inv_z
                b2b_ref[h, lt] = jnp.exp(sa_ref[2 * h + 1, lt] - b[0:1]).astype(BF16)
                rk2b_ref[h, lt] = rc_ref[2 * h + 1, lt].astype(BF16)
            return 0

        lax.fori_loop(0, n_gate // 2, gate_units, 0)
        acc_ref[...] = jnp.zeros(acc_ref.shape, F32)
        for ref in (at0_ref, at1_ref, gt0_ref, gt1_ref):
            ref[...] = jnp.zeros(ref.shape, ref.dtype)

    def half_step(half, tile_b, at_src, gt_dst, at_dst, gt_src):
        e1_base = jnp.clip(tile_b * ke, 0, nk - ke)
        groups = nk // BF16_ROWS
        n_trips = ke // R_PER_PIECE
        a_rows = et // n_trips
        c_rows = d // n_trips

        def trip(i, _):
            cr = pl.ds(pl.multiple_of(i * c_rows, c_rows), c_rows)
            acc_ref[cr, :] += _dot(vt_ref[cr, half * et:(half + 1) * et], gt_src[...])
            a_lo = pl.multiple_of(i * a_rows, a_rows)
            u_rows = pl.ds(pl.multiple_of(half * et + a_lo, a_rows), a_rows)
            at_dst[pl.ds(a_lo, a_rows), :] = _dot(u_ref[u_rows, :], h2_ref[...])

            for lt in range(nlt):
                ls = slice(lt * LANES, (lt + 1) * LANES)
                ws = [[jnp.zeros((BF16_ROWS, LANES), BF16) for _ in range(groups)] for _ in range(R_PER_PIECE)]
                for h in range(P_HEADS):
                    b2 = b2b_ref[h, lt]
                    rk2 = rk2b_ref[h, lt]
                    for rr in range(R_PER_PIECE):
                        e1 = e1_base + i * R_PER_PIECE + rr
                        cnt_row = rc_ref[2 * h, lt, pl.ds(e1, 1), :]
                        a_row = sa_ref[2 * h, lt, pl.ds(e1, 1), :]
                        cnt_t = jnp.broadcast_to(cnt_row, (BF16_ROWS, LANES)).astype(BF16)
                        a_t = jnp.broadcast_to(a_row, (BF16_ROWS, LANES)).astype(BF16)
                        for g in range(groups):
                            gs = slice(g * BF16_ROWS, (g + 1) * BF16_ROWS)
                            ws[rr][g] = ws[rr][g] + a_t * jnp.where(rk2[gs] < cnt_t, b2[gs], jnp.zeros_like(a_t))
                for rr in range(R_PER_PIECE):
                    er = pl.ds(pl.multiple_of((i * R_PER_PIECE + rr) * nk, nk), nk)
                    pre = at_src[er, ls]
                    act = 0.5 * pre * (1.0 + lax.erf(pre * np.float32(2.0 ** -0.5)))
                    gt_dst[er, ls] = act.astype(BF16) * jnp.concatenate(ws[rr], axis=0)
            return 0

        lax.fori_loop(0, n_trips, trip, 0)

    half_step(0, 2 * s - 1, at1_ref, gt1_ref, at0_ref, gt0_ref)
    half_step(1, 2 * s, at0_ref, gt0_ref, at1_ref, gt1_ref)

    @pl.when(s == n_steps - 1)
    def _finish():
        x2 = x_ref[...] + m[:, 5 * d:6 * d] * acc_ref[...].T
        o_ref[...] = _rms(x2) * fg_ref[...]


def _peer_call(x1, mod3, g2, wpqt, keys, u_b, vt_b, fg, tokens_per_batch, tt, et):
    n, d = x1.shape
    ne = u_b.shape[0]
    nk = keys.shape[1]
    nlt = tt // LANES
    n_pairs = ne // (2 * et)
    tiles_per_batch = tokens_per_batch // tt
    unit = (2 * P_HEADS, nlt, nk, LANES)
    return pl.pallas_call(
        _peer_kernel,
        grid=(n // tt, n_pairs + 1),
        in_specs=[pl.BlockSpec((tt, d), lambda t, j: (t, 0)),
                  pl.BlockSpec((1, 1, mod3.shape[-1]), lambda t, j: (t // tiles_per_batch, 0, 0)),
                  _const_spec(g2), _const_spec(wpqt), _const_spec(keys),
                  pl.BlockSpec((2 * et, d), lambda t, j: (jnp.minimum(j, n_pairs - 1), 0)),
                  pl.BlockSpec((d, 2 * et), lambda t, j: (0, jnp.maximum(j - 1, 0))),
                  _const_spec(fg)],
        out_specs=pl.BlockSpec((tt, d), lambda t, j: (t, 0)),
        out_shape=jax.ShapeDtypeStruct((n, d), F32),
        scratch_shapes=[pltpu.VMEM((d, tt), BF16),
                        pltpu.VMEM(unit, F32),
                        pltpu.VMEM(unit, F32),
                        pltpu.VMEM((2 * P_HEADS, nlt, P_TOPK, LANES), F32),
                        pltpu.VMEM((P_HEADS, nlt, nk, LANES), BF16),
                        pltpu.VMEM((P_HEADS, nlt, nk, LANES), BF16),
                        pltpu.VMEM((et, tt), F32), pltpu.VMEM((et, tt), F32),
                        pltpu.VMEM((et, tt), BF16), pltpu.VMEM((et, tt), BF16),
                        pltpu.VMEM((d, tt), F32)],
        compiler_params=_cparams(("parallel", "arbitrary")),
        name="peer",
    )(x1, mod3, g2, wpqt, keys, u_b, vt_b, fg)


def _rope_tables(n_tok):
    rows = n_tok // GRID_W
    r = jnp.repeat(jnp.arange(rows, dtype=jnp.int32), GRID_W)
    col = jnp.tile(jnp.arange(GRID_W, dtype=jnp.int32), rows)
    n_freq = ROPE_DIM // 4
    freq = ROPE_THETA ** (-jnp.arange(n_freq, dtype=F32) / n_freq)
    ang = jnp.stack([r[:, None] * freq, col[:, None] * freq], axis=1)
    dd = np.arange(ROPE_DIM)
    axis, half, fr = dd // (ROPE_DIM // 2), (dd % (ROPE_DIM // 2)) // n_freq, dd % n_freq
    c32 = jnp.cos(ang)[:, axis, fr]
    s32 = jnp.sin(ang)[:, axis, fr] * jnp.where(half == 0, -1.0, 1.0).astype(F32)
    pad = jnp.zeros((n_tok, HEAD_PAD - NOPE_DIM - ROPE_DIM), F32)
    ones = jnp.ones((n_tok, NOPE_DIM), F32)
    ck = jnp.concatenate([ones, c32, pad], axis=1)
    sk = jnp.concatenate([0.0 * ones, s32, pad], axis=1)
    q_scale = ATTN_SCALE * LOG2_E
    return {"cq": ck * q_scale, "sq": sk * q_scale, "ck": ck, "sk": sk}


def _prep_weights(w_in, q_norm_g, w_uq, kv_norm_g, w_ukv, w_o_mla, conv_w, conv_b, conv_ln_g,
                  conv_ln_b, w_pw, w_out):
    d = w_in.shape[0]
    q_rank = q_norm_g.shape[0]
    kv_rank = kv_norm_g.shape[0]
    off_kv = q_rank
    off_kr = q_rank + kv_rank
    off_conv = off_kr + ROPE_DIM
    off_gate = off_conv + 2 * d
    qk = NOPE_DIM + ROPE_DIM
    tail = HEAD_PAD - qk

    def lane_pad(w, before, after):
        return jnp.pad(w, [(0, 0)] * (w.ndim - 1) + [(before, after)])

    def partner(w):
        g = w.reshape(w.shape[:-1] + (2, 2, ROPE_DIM // 4))
        return g[..., ::-1, :].reshape(w.shape)

    def place_rope(w32):
        return lane_pad(w32, NOPE_DIM, tail)

    wkr32 = w_in[:, off_kr:off_conv]
    uq = w_uq.reshape(q_rank, N_HEADS, qk)
    uq_pad = lane_pad(uq, 0, tail)
    uq_sw = lane_pad(partner(uq[:, :, NOPE_DIM:]), NOPE_DIM, tail)
    ukv = w_ukv.reshape(kv_rank, N_HEADS, NOPE_DIM + V_DIM)
    uk_pad = lane_pad(ukv[:, :, :NOPE_DIM], 0, HEAD_PAD - NOPE_DIM)
    uv = ukv[:, :, NOPE_DIM:]
    even = (np.arange(N_HEADS) % 2 == 0)[None, :, None]
    uv_pad = jnp.concatenate([jnp.where(even, uv, 0.0), jnp.where(even, 0.0, uv)], axis=-1)
    hp = N_HEADS * HEAD_PAD
    row = lambda v: v.reshape(1, -1).astype(F32)
    return {
        "wq": w_in[:, :off_kv].astype(BF16),
        "wkvc": w_in[:, off_kv:off_kr].astype(BF16),
        "wkr": place_rope(wkr32).astype(BF16),
        "wkrs": place_rope(partner(wkr32)).astype(BF16),
        "wca": w_in[:, off_conv:off_conv + d].astype(BF16),
        "wcg": w_in[:, off_conv + d:off_gate].astype(BF16),
        "wgate": w_in[:, off_gate:].astype(BF16),
        "qg": row(q_norm_g),
        "wuq": uq_pad.reshape(q_rank, hp).astype(BF16),
        "wuqs": uq_sw.reshape(q_rank, hp).astype(BF16),
        "kvg": row(kv_norm_g),
        "wukv": jnp.concatenate([uk_pad.reshape(kv_rank, hp), uv_pad.reshape(kv_rank, hp)], axis=1).astype(BF16),
        "womla": w_o_mla.astype(BF16),
        "wpw": w_pw.astype(BF16),
        "wout": w_out.astype(BF16),
        "cw": conv_w.astype(F32),
        "cb": row(conv_b),
        "lng": row(conv_ln_g),
        "lnb": row(conv_ln_b),
    }


def _tile_sizes(n_tok):
    return {
        "in_proj": min(512, n_tok),
        "merge": min(256, n_tok),
        "attn_q": min(1024, n_tok),
        "attn_k": min(1024, n_tok),
        "peer_tokens": min(512, n_tok),
        "peer_experts": 512,
    }


def kernel(x, c, ctx, c_ctx, w_mod, b_mod, norm1_g, norm2_g, w_in, q_norm_g, w_uq, kv_norm_g, w_ukv,
           w_o_mla, conv_w, conv_b, conv_ln_g, conv_ln_b, w_pw, w_out, w_pq, sub_keys, u_experts,
           v_experts, final_g):
    assert w_mod.shape[0] == 1, "single-layer configuration"
    b, l, d = x.shape
    assert c.shape[0] + 1 <= 8
    row = lambda v: v.reshape(1, -1).astype(F32)

    cvec = jnp.concatenate([c, c_ctx[None, :], jnp.zeros((SUBLANES - b - 1, d), F32)], axis=0)
    mod = _mod_call(cvec, w_mod[0], b_mod[0].reshape(1, -1))
    mod3 = mod.reshape(8, 1, 6 * d)

    wts = _prep_weights(w_in[0], q_norm_g[0], w_uq[0], kv_norm_g[0], w_ukv[0], w_o_mla[0], conv_w[0],
                        conv_b[0], conv_ln_g[0], conv_ln_b[0], w_pw[0], w_out[0])
    tabs = _rope_tables(l)
    g1 = row(norm1_g[0])

    tiles = _tile_sizes(l)
    q, kl, vl, y, gate = _inproj_call(x, mod3, g1, wts, tabs, tiles["in_proj"])
    kc, vc = _ctxkv_call(ctx, mod3, b, g1, wts)
    attn = _attn_call(q, kl, vl, kc, vc, tq=tiles["attn_q"], tk=tiles["attn_k"])
    x1 = _merge_call(x, mod3, attn, y, gate, wts, tm=tiles["merge"])

    n_keys = sub_keys.shape[3]
    keys = sub_keys[0].reshape(2 * P_HEADS, n_keys, -1).astype(BF16)
    wpqt = w_pq[0].T.astype(BF16)
    u_b = u_experts[0].astype(BF16)
    vt_b = v_experts[0].T.astype(BF16)
    out = _peer_call(x1.reshape(b * l, d), mod3, row(norm2_g[0]), wpqt, keys, u_b, vt_b, row(final_g),
                     tokens_per_batch=l, tt=tiles["peer_tokens"], et=tiles["peer_experts"])
    return out.reshape(b, l, d)
```

```python
import functools

import jax
import jax.numpy as jnp
import numpy as np
from jax import lax
from jax.experimental import pallas as pl
from jax.experimental.pallas import tpu as pltpu

F32 = jnp.float32
BF16 = jnp.bfloat16

EPS = 1e-6
GRID_W = 64
N_HEADS = 16
NOPE_DIM = 64
ROPE_DIM = 32
V_DIM = 64
ROPE_THETA = 10000.0
ATTN_SCALE = (NOPE_DIM + ROPE_DIM) ** -0.5
LOG2_E = 1.4426950408889634
CONV_WIDTH = 31
CONV_HALF = CONV_WIDTH // 2
P_HEADS = 8
P_TOPK = 16

LANES = 128
SUBLANES = 8
BF16_ROWS = 16
HEAD_PAD = LANES
HALO = 16
R_PER_PIECE = 2
VMEM_LIMIT = 56 * 1024 * 1024

_NT = (((1,), (1,)), ((), ()))


def _dot(a, b):
    return jnp.dot(a, b, preferred_element_type=F32)


def _dot_nt(a, b):
    return lax.dot_general(a, b, _NT, preferred_element_type=F32)


def _rms(x):
    return x * lax.rsqrt(jnp.mean(x * x, axis=-1, keepdims=True) + EPS)


def _cparams(sem):
    return pltpu.CompilerParams(dimension_semantics=sem, vmem_limit_bytes=VMEM_LIMIT)


def _mod_kernel(c_ref, w_ref, b_ref, o_ref):
    cv = c_ref[...]
    s = cv * jax.nn.sigmoid(cv)
    o_ref[...] = jnp.dot(s, w_ref[...], preferred_element_type=F32,
                         precision=lax.Precision.HIGHEST) + b_ref[...]


def _mod_call(cvec, w_mod, b_mod):
    rows, d = cvec.shape
    n = w_mod.shape[1]
    tn = d
    return pl.pallas_call(
        _mod_kernel,
        grid=(n // tn,),
        in_specs=[pl.BlockSpec((rows, d), lambda j: (0, 0)),
                  pl.BlockSpec((d, tn), lambda j: (0, j)),
                  pl.BlockSpec((1, tn), lambda j: (0, j))],
        out_specs=pl.BlockSpec((rows, tn), lambda j: (0, j)),
        out_shape=jax.ShapeDtypeStruct((rows, n), F32),
        compiler_params=_cparams(("arbitrary",)),
        name="mod",
    )(cvec, w_mod, b_mod)


def _modulated_norm(x, g, shift, scale):
    return (_rms(x) * g) * (1.0 + scale) + shift


def _kv_from_h(hb, wkvc_ref, kvg_ref, wukv_ref):
    ckv = _dot(hb, wkvc_ref[...])
    ckvn = (_rms(ckv) * kvg_ref[...]).astype(BF16)
    return _dot(ckvn, wukv_ref[...])


def _inproj_kernel(x_ref, mod_ref, g1_ref, wq_ref, wkvc_ref, wkr_ref, wkrs_ref, wca_ref, wcg_ref,
                   wgate_ref, qg_ref, wuq_ref, wuqs_ref, kvg_ref, wukv_ref,
                   cq_ref, sq_ref, ck_ref, sk_ref,
                   q_out, k_out, v_out, y_out, gate_out):
    d = x_ref.shape[-1]
    hp = N_HEADS * HEAD_PAD
    m = mod_ref[0]
    hb = _modulated_norm(x_ref[0], g1_ref[...], m[:, 0:d], m[:, d:2 * d]).astype(BF16)

    cqn = (_rms(_dot(hb, wq_ref[...])) * qg_ref[...]).astype(BF16)
    qa = _dot(cqn, wuq_ref[...])
    qb = _dot(cqn, wuqs_ref[...])
    cq, sq = cq_ref[...], sq_ref[...]
    for h in range(N_HEADS):
        sl = slice(h * HEAD_PAD, (h + 1) * HEAD_PAD)
        q_out[0, :, sl] = (qa[:, sl] * cq + qb[:, sl] * sq).astype(BF16)

    kv = _kv_from_h(hb, wkvc_ref, kvg_ref, wukv_ref)
    kr = _dot(hb, wkr_ref[...]) * ck_ref[...] + _dot(hb, wkrs_ref[...]) * sk_ref[...]
    for h in range(N_HEADS):
        sl = slice(h * HEAD_PAD, (h + 1) * HEAD_PAD)
        k_out[0, :, sl] = (kv[:, sl] + kr).astype(BF16)
    v_out[0] = kv[:, hp:].astype(BF16)

    a = _dot(hb, wca_ref[...])
    g = _dot(hb, wcg_ref[...])
    y_out[0] = (a * jax.nn.sigmoid(g)).astype(BF16)
    gate_out[0] = jax.nn.sigmoid(_dot(hb, wgate_ref[...])).astype(BF16)


def _const_spec(arr):
    nd = arr.ndim
    return pl.BlockSpec(arr.shape, lambda *_: (0,) * nd)


def _inproj_call(x, mod3, g1, wts, tabs, tm):
    b, l, d = x.shape
    hp = N_HEADS * HEAD_PAD
    nt = l // tm
    weights = [wts[k] for k in ("wq", "wkvc", "wkr", "wkrs", "wca", "wcg", "wgate",
                                "qg", "wuq", "wuqs", "kvg", "wukv")]
    tok = lambda w: pl.BlockSpec((1, tm, w), lambda bi, i: (bi, i, 0))
    tab = pl.BlockSpec((tm, HEAD_PAD), lambda bi, i: (i, 0))
    return pl.pallas_call(
        _inproj_kernel,
        grid=(b, nt),
        in_specs=[tok(d),
                  pl.BlockSpec((1, 1, mod3.shape[-1]), lambda bi, i: (bi, 0, 0)),
                  _const_spec(g1)] + [_const_spec(w) for w in weights] + [tab] * 4,
        out_specs=[tok(hp), tok(hp), tok(hp), tok(d), tok(2 * d)],
        out_shape=[jax.ShapeDtypeStruct((b, l, hp), BF16),
                   jax.ShapeDtypeStruct((b, l, hp), BF16),
                   jax.ShapeDtypeStruct((b, l, hp), BF16),
                   jax.ShapeDtypeStruct((b, l, d), BF16),
                   jax.ShapeDtypeStruct((b, l, 2 * d), BF16)],
        compiler_params=_cparams(("parallel", "parallel")),
        name="in_proj",
    )(x, mod3, g1, *weights, tabs["cq"], tabs["sq"], tabs["ck"], tabs["sk"])


def _ctxkv_kernel(x_ref, mod_ref, g1_ref, wkvc_ref, wkr_ref, kvg_ref, wukv_ref, k_out, v_out):
    d = x_ref.shape[-1]
    hp = N_HEADS * HEAD_PAD
    m = mod_ref[0]
    hb = _modulated_norm(x_ref[0], g1_ref[...], m[:, 0:d], m[:, d:2 * d]).astype(BF16)
    kv = _kv_from_h(hb, wkvc_ref, kvg_ref, wukv_ref)
    kr = _dot(hb, wkr_ref[...])
    for h in range(N_HEADS):
        sl = slice(h * HEAD_PAD, (h + 1) * HEAD_PAD)
        k_out[0, :, sl] = (kv[:, sl] + kr).astype(BF16)
    v_out[0] = kv[:, hp:].astype(BF16)


def _ctxkv_call(ctx, mod3, ctx_row, g1, wts):
    b, n, d = ctx.shape
    hp = N_HEADS * HEAD_PAD
    weights = [wts[k] for k in ("wkvc", "wkr", "kvg", "wukv")]
    tok = lambda w: pl.BlockSpec((1, n, w), lambda bi: (bi, 0, 0))
    return pl.pallas_call(
        _ctxkv_kernel,
        grid=(b,),
        in_specs=[tok(d),
                  pl.BlockSpec((1, 1, mod3.shape[-1]), lambda bi: (ctx_row, 0, 0)),
                  _const_spec(g1)] + [_const_spec(w) for w in weights],
        out_specs=[tok(hp), tok(hp)],
        out_shape=[jax.ShapeDtypeStruct((b, n, hp), BF16)] * 2,
        compiler_params=_cparams(("parallel",)),
        name="ctx_kv",
    )(ctx, mod3, g1, *weights)


def _attn_kernel(q_ref, kl_ref, vl_ref, kc_ref, vc_ref, o_ref, m_ref, l_ref, acc_ref, *, tk):
    n_chunks = kl_ref.shape[1] // tk
    m_ref[...] = jnp.full(m_ref.shape, -jnp.inf, F32)
    l_ref[...] = jnp.zeros(l_ref.shape, F32)
    acc_ref[...] = jnp.zeros(acc_ref.shape, F32)

    def update(head, k, v):
        sl = slice(head * HEAD_PAD, (head + 1) * HEAD_PAD)
        s = _dot_nt(q_ref[0, :, sl], k)
        tiles = [s[:, t * LANES:(t + 1) * LANES] for t in range(s.shape[1] // LANES)]
        tile_max = functools.reduce(jnp.maximum, tiles)
        m_old = m_ref[head]
        m_new = jnp.maximum(m_old, jnp.max(tile_max, axis=-1, keepdims=True))
        alpha = jnp.exp2(m_old - m_new)
        ps = [jnp.exp2(t - m_new) for t in tiles]
        l_ref[head] = alpha * l_ref[head] + functools.reduce(jnp.add, ps)
        p = jnp.concatenate([t.astype(BF16) for t in ps], axis=1)
        acc_ref[head] = alpha * acc_ref[head] + _dot(p, v)
        m_ref[head] = m_new

    def body(c, _):
        off = pl.multiple_of(c * tk, tk)
        for head in range(2):
            sl = slice(head * HEAD_PAD, (head + 1) * HEAD_PAD)
            update(head, kl_ref[0, pl.ds(off, tk), sl], vl_ref[0, pl.ds(off, tk), sl])
        return 0

    lax.fori_loop(0, n_chunks, body, 0)
    out = None
    for head in range(2):
        sl = slice(head * HEAD_PAD, (head + 1) * HEAD_PAD)
        update(head, kc_ref[0, :, sl], vc_ref[0, :, sl])
        o = acc_ref[head] * (1.0 / jnp.sum(l_ref[head], axis=-1, keepdims=True))
        out = o if out is None else out + o
    o_ref[0] = out.astype(BF16)


def _attn_call(q, kl, vl, kc, vc, tq, tk):
    b, l, hp = q.shape
    nc = kc.shape[1]
    pairs = N_HEADS // 2
    pw = 2 * HEAD_PAD
    return pl.pallas_call(
        functools.partial(_attn_kernel, tk=tk),
        grid=(b, pairs, l // tq),
        in_specs=[pl.BlockSpec((1, tq, pw), lambda bi, p, i: (bi, i, p)),
                  pl.BlockSpec((1, l, pw), lambda bi, p, i: (bi, 0, p)),
                  pl.BlockSpec((1, l, pw), lambda bi, p, i: (bi, 0, p)),
                  pl.BlockSpec((1, nc, pw), lambda bi, p, i: (bi, 0, p)),
                  pl.BlockSpec((1, nc, pw), lambda bi, p, i: (bi, 0, p))],
        out_specs=pl.BlockSpec((1, tq, HEAD_PAD), lambda bi, p, i: (bi, i, p)),
        out_shape=jax.ShapeDtypeStruct((b, l, pairs * HEAD_PAD), BF16),
        scratch_shapes=[pltpu.VMEM((2, tq, HEAD_PAD), F32)] * 3,
        compiler_params=_cparams(("parallel", "parallel", "parallel")),
        name="attention",
    )(q, kl, vl, kc, vc)


def _merge_kernel(x_ref, mod_ref, attn_ref, y_ref, yp_ref, yn_ref, gate_ref,
                  womla_ref, wpw_ref, wout_ref, cw_ref, cb_ref, lng_ref, lnb_ref,
                  o_ref, ybuf, sbuf, cbuf, *, rows):
    i = pl.program_id(1)
    last = pl.num_programs(1) - 1
    tm, d = y_ref.shape[1], y_ref.shape[2]

    prev = yp_ref[0].astype(F32)
    nxt = yn_ref[0].astype(F32)
    ybuf[0:HALO, :] = jnp.where(i > 0, prev, 0.0)
    ybuf[HALO:HALO + tm, :] = y_ref[0].astype(F32)
    ybuf[HALO + tm:2 * HALO + tm, :] = jnp.where(i < last, nxt, 0.0)
    span = sbuf.shape[1]
    for p in range(SUBLANES):
        sbuf[p] = ybuf[p:p + span, :]

    cb = cb_ref[...]

    def conv_rows(r, _):
        base = pl.multiple_of(r * rows, rows)
        acc = jnp.broadcast_to(cb, (rows, d))
        for k in range(CONV_WIDTH):
            q, p = divmod(HALO - CONV_HALF + k, SUBLANES)
            acc = acc + cw_ref[k:k + 1, :] * sbuf[p, pl.ds(base + q * SUBLANES, rows), :]
        cbuf[pl.ds(base, rows), :] = acc
        return 0

    lax.fori_loop(0, tm // rows, conv_rows, 0)

    yc = cbuf[...]
    mu = jnp.mean(yc, axis=-1, keepdims=True)
    cen = yc - mu
    var = jnp.mean(cen * cen, axis=-1, keepdims=True)
    ln = cen * lax.rsqrt(var + EPS) * lng_ref[...] + lnb_ref[...]
    act = (ln * jax.nn.sigmoid(ln)).astype(BF16)
    y_c = _dot(act, wpw_ref[...])
    y_a = _dot(attn_ref[0], womla_ref[...])
    gate = gate_ref[0]
    z = gate[:, :d].astype(F32) * y_a + gate[:, d:].astype(F32) * y_c
    zo = _dot(z.astype(BF16), wout_ref[...])
    m = mod_ref[0]
    o_ref[0] = x_ref[0] + m[:, 2 * d:3 * d] * zo


def _merge_call(x, mod3, attn, y, gate, wts, tm):
    b, l, d = x.shape
    nt = l // tm
    hb = tm // HALO
    nhb = l // HALO
    weights = [wts[k] for k in ("womla", "wpw", "wout", "cw", "cb", "lng", "lnb")]
    tok = lambda w: pl.BlockSpec((1, tm, w), lambda bi, i: (bi, i, 0))
    return pl.pallas_call(
        functools.partial(_merge_kernel, rows=32),
        grid=(b, nt),
        in_specs=[tok(d),
                  pl.BlockSpec((1, 1, mod3.shape[-1]), lambda bi, i: (bi, 0, 0)),
                  tok(d), tok(d),
                  pl.BlockSpec((1, HALO, d), lambda bi, i: (bi, jnp.maximum(i * hb - 1, 0), 0)),
                  pl.BlockSpec((1, HALO, d), lambda bi, i: (bi, jnp.minimum((i + 1) * hb, nhb - 1), 0)),
                  tok(2 * d)] + [_const_spec(w) for w in weights],
        out_specs=tok(d),
        out_shape=jax.ShapeDtypeStruct((b, l, d), F32),
        scratch_shapes=[pltpu.VMEM((tm + 2 * HALO, d), F32),
                        pltpu.VMEM((SUBLANES, tm + 2 * HALO - SUBLANES, d), F32),
                        pltpu.VMEM((tm, d), F32)],
        compiler_params=_cparams(("parallel", "parallel")),
        name="merge",
    )(x, mod3, attn, y, y, y, gate, *weights)


def _topk_rank(s, exact):
    nk = s.shape[0]
    key_id = lax.broadcasted_iota(jnp.int32, s.shape, 0)
    rank = jnp.full(s.shape, float(P_TOPK), F32)
    top_id = lax.broadcasted_iota(jnp.int32, (P_TOPK, s.shape[1]), 0)
    tops = jnp.zeros((P_TOPK, s.shape[1]), F32)
    for r in range(P_TOPK):
        cur = jnp.max(s, axis=0, keepdims=True)
        hit = s == cur
        if exact:
            first = jnp.min(jnp.where(hit, key_id, nk), axis=0, keepdims=True)
            hit = key_id == first
        rank = jnp.where(hit, float(r), rank)
        s = jnp.where(hit, -jnp.inf, s)
        tops = jnp.where(top_id == r, cur, tops)
    return tops, rank


def _rank_is_clean(rank):
    kept = jnp.sum(jnp.where(rank < float(P_TOPK), 1.0, 0.0), axis=0, keepdims=True)
    return jnp.max(jnp.abs(kept - float(P_TOPK))) == 0.0


def _staircase(a, b):
    half = P_TOPK // 2
    a_lo, a_hi = a[:half], a[half:]
    row_lo = lax.broadcasted_iota(jnp.int32, a_lo.shape, 0)
    row_hi = row_lo + half
    cnt_lo = jnp.zeros(a_lo.shape, F32)
    cnt_hi = jnp.zeros(a_hi.shape, F32)
    f_lo = a_lo + b[0:1]
    f_hi = a_hi + b[0:1]
    for _ in range(P_TOPK):
        best = jnp.max(jnp.maximum(f_lo, f_hi), axis=0, keepdims=True)
        cand = jnp.minimum(jnp.where(f_lo == best, row_lo, P_TOPK), jnp.where(f_hi == best, row_hi, P_TOPK))
        pick = jnp.min(cand, axis=0, keepdims=True)
        hit_lo = row_lo == pick
        hit_hi = row_hi == pick
        cnt_lo = jnp.where(hit_lo, cnt_lo + 1.0, cnt_lo)
        cnt_hi = jnp.where(hit_hi, cnt_hi + 1.0, cnt_hi)
        nxt = jnp.full(a_lo.shape, -jnp.inf, F32)
        for k in range(1, P_TOPK):
            nxt = jnp.where(cnt_lo == float(k), b[k:k + 1], nxt)
        f_lo = jnp.where(hit_lo, a_lo + nxt, f_lo)
        f_hi = jnp.where(hit_hi, -jnp.inf, f_hi)
    return jnp.concatenate([cnt_lo, cnt_hi], axis=0)


def _peer_kernel(x_ref, mod_ref, g2_ref, wpqt_ref, keys_ref, u_ref, vt_ref, fg_ref, o_ref,
                 h2_ref, sa_ref, rc_ref, top_ref, b2b_ref, rk2b_ref, at0_ref, at1_ref, gt0_ref, gt1_ref,
                 acc_ref):
    s = pl.program_id(1)
    n_steps = pl.num_programs(1)
    tt, d = x_ref.shape
    nlt = tt // LANES
    nk = keys_ref.shape[1]
    et = u_ref.shape[0] // 2
    ke = et // nk
    m = mod_ref[0]

    @pl.when(s == 0)
    def _prepare():
        h2 = _modulated_norm(x_ref[...], g2_ref[...], m[:, 3 * d:4 * d], m[:, 4 * d:5 * d])
        h2_ref[...] = h2.T.astype(BF16)
        dk = keys_ref.shape[2]
        qt = _dot(wpqt_ref[...], h2_ref[...]).astype(BF16)
        for hp in range(2 * P_HEADS):
            sc = _dot(keys_ref[hp], qt[hp * dk:(hp + 1) * dk, :])
            for lt in range(nlt):
                sa_ref[hp, lt] = sc[:, lt * LANES:(lt + 1) * LANES]

        n_rank = 2 * P_HEADS * nlt

        def rank_units(u, _):
            units = []
            for v in (u, u + n_rank // 4, u + n_rank // 2, u + 3 * n_rank // 4):
                hp, lt = v // nlt, v % nlt
                tops, rank = _topk_rank(sa_ref[hp, lt], exact=False)
                top_ref[hp, lt] = tops
                rc_ref[hp, lt] = rank
                units.append((hp, lt, _rank_is_clean(rank)))
            for hp, lt, clean in units:
                @pl.when(jnp.logical_not(clean))
                def _redo(hp=hp, lt=lt):
                    tops, rank = _topk_rank(sa_ref[hp, lt], exact=True)
                    top_ref[hp, lt] = tops
                    rc_ref[hp, lt] = rank
            return 0

        lax.fori_loop(0, n_rank // 4, rank_units, 0)

        n_gate = P_HEADS * nlt

        def gate_units(u, _):
            for v in (u, u + n_gate // 4, u + n_gate // 2, u + 3 * n_gate // 4):
                h, lt = v // nlt, v % nlt
                a, b = top_ref[2 * h, lt], top_ref[2 * h + 1, lt]
                cnt = _staircase(a, b)
                ea = jnp.exp(a - a[0:1])
                eb = jnp.exp(b - b[0:1])
                cum = jnp.zeros(a.shape, F32)
                for k in range(P_TOPK):
                    cum = cum + jnp.where(cnt > float(k), eb[k:k + 1], 0.0)
                inv_z = 1.0 / jnp.sum(ea * cum, axis=0, keepdims=True)
                r1 = rc_ref[2 * h, lt]
                cnt1 = jnp.zeros(r1.shape, F32)
                for k in range(P_TOPK):
                    cnt1 = jnp.where(r1 == float(k), cnt[k:k + 1], cnt1)
                rc_ref[2 * h, lt] = cnt1
                sa_ref[2 * h, lt] = jnp.exp(sa_ref[2 * h, lt] - a[0:1]) * inv_z
                b2b_ref[h, lt] = jnp.exp(sa_ref[2 * h + 1, lt] - b[0:1]).astype(BF16)
                rk2b_ref[h, lt] = rc_ref[2 * h + 1, lt].astype(BF16)
            return 0

        lax.fori_loop(0, n_gate // 4, gate_units, 0)
        acc_ref[...] = jnp.zeros(acc_ref.shape, F32)
        for ref in (at0_ref, at1_ref, gt0_ref, gt1_ref):
            ref[...] = jnp.zeros(ref.shape, ref.dtype)

    def half_step(half, tile_b, at_src, gt_dst, at_dst, gt_src):
        e1_base = jnp.clip(tile_b * ke, 0, nk - ke)
        groups = nk // BF16_ROWS
        n_trips = ke // R_PER_PIECE
        a_rows = et // n_trips
        c_rows = d // n_trips

        def trip(i, _):
            cr = pl.ds(pl.multiple_of(i * c_rows, c_rows), c_rows)
            acc_ref[cr, :] += _dot(vt_ref[cr, half * et:(half + 1) * et], gt_src[...])
            a_lo = pl.multiple_of(i * a_rows, a_rows)
            u_rows = pl.ds(pl.multiple_of(half * et + a_lo, a_rows), a_rows)
            at_dst[pl.ds(a_lo, a_rows), :] = _dot(u_ref[u_rows, :], h2_ref[...])

            for lt in range(nlt):
                ls = slice(lt * LANES, (lt + 1) * LANES)
                ws = [[jnp.zeros((BF16_ROWS, LANES), BF16) for _ in range(groups)] for _ in range(R_PER_PIECE)]
                for h in range(P_HEADS):
                    b2 = b2b_ref[h, lt]
                    rk2 = rk2b_ref[h, lt]
                    for rr in range(R_PER_PIECE):
                        e1 = e1_base + i * R_PER_PIECE + rr
                        cnt_row = rc_ref[2 * h, lt, pl.ds(e1, 1), :]
                        a_row = sa_ref[2 * h, lt, pl.ds(e1, 1), :]
                        cnt_t = jnp.broadcast_to(cnt_row, (BF16_ROWS, LANES)).astype(BF16)
                        a_t = jnp.broadcast_to(a_row, (BF16_ROWS, LANES)).astype(BF16)
                        for g in range(groups):
                            gs = slice(g * BF16_ROWS, (g + 1) * BF16_ROWS)
                            ws[rr][g] = ws[rr][g] + a_t * jnp.where(rk2[gs] < cnt_t, b2[gs], jnp.zeros_like(a_t))
                for rr in range(R_PER_PIECE):
                    er = pl.ds(pl.multiple_of((i * R_PER_PIECE + rr) * nk, nk), nk)
                    pre = at_src[er, ls]
                    act = 0.5 * pre * (1.0 + lax.erf(pre * np.float32(2.0 ** -0.5)))
                    gt_dst[er, ls] = act.astype(BF16) * jnp.concatenate(ws[rr], axis=0)
            return 0

        lax.fori_loop(0, n_trips, trip, 0)

    half_step(0, 2 * s - 1, at1_ref, gt1_ref, at0_ref, gt0_ref)
    half_step(1, 2 * s, at0_ref, gt0_ref, at1_ref, gt1_ref)

    @pl.when(s == n_steps - 1)
    def _finish():
        x2 = x_ref[...] + m[:, 5 * d:6 * d] * acc_ref[...].T
        o_ref[...] = _rms(x2) * fg_ref[...]


def _peer_call(x1, mod3, g2, wpqt, keys, u_b, vt_b, fg, tokens_per_batch, tt, et):
    n, d = x1.shape
    ne = u_b.shape[0]
    nk = keys.shape[1]
    nlt = tt // LANES
    n_pairs = ne // (2 * et)
    tiles_per_batch = tokens_per_batch // tt
    unit = (2 * P_HEADS, nlt, nk, LANES)
    return pl.pallas_call(
        _peer_kernel,
        grid=(n // tt, n_pairs + 1),
        in_specs=[pl.BlockSpec((tt, d), lambda t, j: (t, 0)),
                  pl.BlockSpec((1, 1, mod3.shape[-1]), lambda t, j: (t // tiles_per_batch, 0, 0)),
                  _const_spec(g2), _const_spec(wpqt), _const_spec(keys),
                  pl.BlockSpec((2 * et, d), lambda t, j: (jnp.minimum(j, n_pairs - 1), 0)),
                  pl.BlockSpec((d, 2 * et), lambda t, j: (0, jnp.maximum(j - 1, 0))),
                  _const_spec(fg)],
        out_specs=pl.BlockSpec((tt, d), lambda t, j: (t, 0)),
        out_shape=jax.ShapeDtypeStruct((n, d), F32),
        scratch_shapes=[pltpu.VMEM((d, tt), BF16),
                        pltpu.VMEM(unit, F32),
                        pltpu.VMEM(unit, F32),
                        pltpu.VMEM((2 * P_HEADS, nlt, P_TOPK, LANES), F32),
                        pltpu.VMEM((P_HEADS, nlt, nk, LANES), BF16),
                        pltpu.VMEM((P_HEADS, nlt, nk, LANES), BF16),
                        pltpu.VMEM((et, tt), F32), pltpu.VMEM((et, tt), F32),
                        pltpu.VMEM((et, tt), BF16), pltpu.VMEM((et, tt), BF16),
                        pltpu.VMEM((d, tt), F32)],
        compiler_params=_cparams(("parallel", "arbitrary")),
        name="peer",
    )(x1, mod3, g2, wpqt, keys, u_b, vt_b, fg)


def _rope_tables(n_tok):
    rows = n_tok // GRID_W
    r = jnp.repeat(jnp.arange(rows, dtype=jnp.int32), GRID_W)
    col = jnp.tile(jnp.arange(GRID_W, dtype=jnp.int32), rows)
    n_freq = ROPE_DIM // 4
    freq = ROPE_THETA ** (-jnp.arange(n_freq, dtype=F32) / n_freq)
    ang = jnp.stack([r[:, None] * freq, col[:, None] * freq], axis=1)
    dd = np.arange(ROPE_DIM)
    axis, half, fr = dd // (ROPE_DIM // 2), (dd % (ROPE_DIM // 2)) // n_freq, dd % n_freq
    c32 = jnp.cos(ang)[:, axis, fr]
    s32 = jnp.sin(ang)[:, axis, fr] * jnp.where(half == 0, -1.0, 1.0).astype(F32)
    pad = jnp.zeros((n_tok, HEAD_PAD - NOPE_DIM - ROPE_DIM), F32)
    ones = jnp.ones((n_tok, NOPE_DIM), F32)
    ck = jnp.concatenate([ones, c32, pad], axis=1)
    sk = jnp.concatenate([0.0 * ones, s32, pad], axis=1)
    q_scale = ATTN_SCALE * LOG2_E
    return {"cq": ck * q_scale, "sq": sk * q_scale, "ck": ck, "sk": sk}


def _prep_weights(w_in, q_norm_g, w_uq, kv_norm_g, w_ukv, w_o_mla, conv_w, conv_b, conv_ln_g,
                  conv_ln_b, w_pw, w_out):
    d = w_in.shape[0]
    q_rank = q_norm_g.shape[0]
    kv_rank = kv_norm_g.shape[0]
    off_kv = q_rank
    off_kr = q_rank + kv_rank
    off_conv = off_kr + ROPE_DIM
    off_gate = off_conv + 2 * d
    qk = NOPE_DIM + ROPE_DIM
    tail = HEAD_PAD - qk

    def lane_pad(w, before, after):
        return jnp.pad(w, [(0, 0)] * (w.ndim - 1) + [(before, after)])

    def partner(w):
        g = w.reshape(w.shape[:-1] + (2, 2, ROPE_DIM // 4))
        return g[..., ::-1, :].reshape(w.shape)

    def place_rope(w32):
        return lane_pad(w32, NOPE_DIM, tail)

    wkr32 = w_in[:, off_kr:off_conv]
    uq = w_uq.reshape(q_rank, N_HEADS, qk)
    uq_pad = lane_pad(uq, 0, tail)
    uq_sw = lane_pad(partner(uq[:, :, NOPE_DIM:]), NOPE_DIM, tail)
    ukv = w_ukv.reshape(kv_rank, N_HEADS, NOPE_DIM + V_DIM)
    uk_pad = lane_pad(ukv[:, :, :NOPE_DIM], 0, HEAD_PAD - NOPE_DIM)
    uv = ukv[:, :, NOPE_DIM:]
    even = (np.arange(N_HEADS) % 2 == 0)[None, :, None]
    uv_pad = jnp.concatenate([jnp.where(even, uv, 0.0), jnp.where(even, 0.0, uv)], axis=-1)
    hp = N_HEADS * HEAD_PAD
    row = lambda v: v.reshape(1, -1).astype(F32)
    return {
        "wq": w_in[:, :off_kv].astype(BF16),
        "wkvc": w_in[:, off_kv:off_kr].astype(BF16),
        "wkr": place_rope(wkr32).astype(BF16),
        "wkrs": place_rope(partner(wkr32)).astype(BF16),
        "wca": w_in[:, off_conv:off_conv + d].astype(BF16),
        "wcg": w_in[:, off_conv + d:off_gate].astype(BF16),
        "wgate": w_in[:, off_gate:].astype(BF16),
        "qg": row(q_norm_g),
        "wuq": uq_pad.reshape(q_rank, hp).astype(BF16),
        "wuqs": uq_sw.reshape(q_rank, hp).astype(BF16),
        "kvg": row(kv_norm_g),
        "wukv": jnp.concatenate([uk_pad.reshape(kv_rank, hp), uv_pad.reshape(kv_rank, hp)], axis=1).astype(BF16),
        "womla": w_o_mla.astype(BF16),
        "wpw": w_pw.astype(BF16),
        "wout": w_out.astype(BF16),
        "cw": conv_w.astype(F32),
        "cb": row(conv_b),
        "lng": row(conv_ln_g),
        "lnb": row(conv_ln_b),
    }


def _tile_sizes(n_tok):
    return {
        "in_proj": min(512, n_tok),
        "merge": min(256, n_tok),
        "attn_q": min(1024, n_tok),
        "attn_k": min(1024, n_tok),
        "peer_tokens": min(512, n_tok),
        "peer_experts": 512,
    }


def kernel(x, c, ctx, c_ctx, w_mod, b_mod, norm1_g, norm2_g, w_in, q_norm_g, w_uq, kv_norm_g, w_ukv,
           w_o_mla, conv_w, conv_b, conv_ln_g, conv_ln_b, w_pw, w_out, w_pq, sub_keys, u_experts,
           v_experts, final_g):
    assert w_mod.shape[0] == 1, "single-layer configuration"
    b, l, d = x.shape
    assert c.shape[0] + 1 <= 8
    row = lambda v: v.reshape(1, -1).astype(F32)

    cvec = jnp.concatenate([c, c_ctx[None, :], jnp.zeros((SUBLANES - b - 1, d), F32)], axis=0)
    mod = _mod_call(cvec, w_mod[0], b_mod[0].reshape(1, -1))
    mod3 = mod.reshape(8, 1, 6 * d)

    wts = _prep_weights(w_in[0], q_norm_g[0], w_uq[0], kv_norm_g[0], w_ukv[0], w_o_mla[0], conv_w[0],
                        conv_b[0], conv_ln_g[0], conv_ln_b[0], w_pw[0], w_out[0])
    tabs = _rope_tables(l)
    g1 = row(norm1_g[0])

    tiles = _tile_sizes(l)
    q, kl, vl, y, gate = _inproj_call(x, mod3, g1, wts, tabs, tiles["in_proj"])
    kc, vc = _ctxkv_call(ctx, mod3, b, g1, wts)
    attn = _attn_call(q, kl, vl, kc, vc, tq=tiles["attn_q"], tk=tiles["attn_k"])
    x1 = _merge_call(x, mod3, attn, y, gate, wts, tm=tiles["merge"])

    n_keys = sub_keys.shape[3]
    keys = sub_keys[0].reshape(2 * P_HEADS, n_keys, -1).astype(BF16)
    wpqt = w_pq[0].T.astype(BF16)
    u_b = u_experts[0].astype(BF16)
    vt_b = v_experts[0].T.astype(BF16)
    out = _peer_call(x1.reshape(b * l, d), mod3, row(norm2_g[0]), wpqt, keys, u_b, vt_b, row(final_g),
                     tokens_per_batch=l, tt=tiles["peer_tokens"], et=tiles["peer_experts"])
    return out.reshape(b, l, d)
```
